```python
import jax, jax.numpy as jnp
from jax import lax
import numpy as np

D_MODEL = 1024
BATCH = 8
SEQ = 4096
DEPTH = 2
DEC_BATCH = 32
DEC_SEQ = 2048
PAST_LEN = 128

N_META = 16
BLOCK = 128
WINDOW = 128
N_HEADS = 8
N_KV_HEADS = 2
GQA_GROUP = N_HEADS // N_KV_HEADS
HEAD_DIM = 64
Q_DIM = N_HEADS * HEAD_DIM
KV_DIM = N_KV_HEADS * HEAD_DIM
POOL_DIM = D_MODEL // 2
POOL_WINDOWS = (2, 4, 8, 16)
N_POOL_GROUPS = len(POOL_WINDOWS)
POOL_GROUP = POOL_DIM // N_POOL_GROUPS
IN_DIM = Q_DIM + 2 * KV_DIM + POOL_DIM
MIX_DIM = Q_DIM + POOL_DIM
N_FOURIER_GROUPS = 4
FOURIER_GROUP = D_MODEL // N_FOURIER_GROUPS
D_FF = 2816
N_EVEN = (DEPTH + 1) // 2
N_ODD = DEPTH // 2
EPS = 1e-6
NEG_INF = -1e30

kernel_name = "hybrid_bidir_swa_pool_fnet_encoder"


def rms_norm(x, g):
    xf = x.astype(jnp.float32)
    y = xf * lax.rsqrt(jnp.mean(xf * xf, axis=-1, keepdims=True) + EPS)
    return (y * g.astype(jnp.float32)).astype(x.dtype)


def swiglu(x, w_gate, w_up, w_down):
    return (jax.nn.silu(x @ w_gate) * (x @ w_up)) @ w_down


def alibi_slopes():
    h = jnp.arange(1, N_HEADS + 1, dtype=jnp.float32)
    return jnp.exp2(-8.0 * h / N_HEADS)


def window_attention(q, k, v, sink):
    B, L = q.shape[:2]
    pad_front = BLOCK - N_META
    Lp = L + pad_front
    nb = Lp // BLOCK
    qp = jnp.pad(q, ((0, 0), (pad_front, 0), (0, 0), (0, 0)))
    qp = qp.reshape(B, nb, BLOCK, N_KV_HEADS, GQA_GROUP, HEAD_DIM)

    def kv_windows(a):
        ap = jnp.pad(a, ((0, 0), (pad_front + BLOCK, BLOCK), (0, 0), (0, 0)))
        ap = ap.reshape(B, nb + 2, BLOCK, N_KV_HEADS, HEAD_DIM)
        return jnp.concatenate([ap[:, :-2], ap[:, 1:-1], ap[:, 2:]], axis=2)

    kw = kv_windows(k)
    vw = kv_windows(v)
    scale = HEAD_DIM ** -0.5
    s = jnp.einsum('bnqkgd,bnskd->bnkgqs', qp, kw).astype(jnp.float32) * scale

    qpos = jnp.arange(nb)[:, None] * BLOCK + jnp.arange(BLOCK)[None, :]
    kpos = (jnp.arange(nb)[:, None] - 1) * BLOCK + jnp.arange(3 * BLOCK)[None, :]
    dist = jnp.abs(qpos[:, :, None] - kpos[:, None, :])
    valid = (dist <= WINDOW) & (kpos[:, None, :] >= pad_front) & (kpos[:, None, :] < Lp)
    slopes = alibi_slopes().reshape(N_KV_HEADS, GQA_GROUP)
    bias = -slopes[None, :, :, None, None] * dist[:, None, None].astype(jnp.float32)
    s = jnp.where(valid[:, None, None], s + bias[None], NEG_INF)

    sk = sink.astype(jnp.float32).reshape(N_KV_HEADS, GQA_GROUP)[None, None, :, :, None]
    m = jnp.maximum(jnp.max(s, axis=-1), sk)
    p = jnp.exp(s - m[..., None])
    denom = jnp.sum(p, axis=-1) + jnp.exp(sk - m)
    o = jnp.einsum('bnkgqs,bnskd->bnkgqd', p.astype(v.dtype), vw).astype(jnp.float32)
    o = (o / denom[..., None]).astype(q.dtype)
    o = o.transpose(0, 1, 4, 2, 3, 5).reshape(B, Lp, Q_DIM)
    return o[:, pad_front:]


def multiscale_pool(u, w_pool, pool_scale):
    B, L, _ = u.shape
    uf = u.astype(jnp.float32).reshape(B, L, N_POOL_GROUPS, POOL_GROUP)
    c = jnp.concatenate([jnp.zeros((B, 1, N_POOL_GROUPS, POOL_GROUP), jnp.float32),
                         jnp.cumsum(uf, axis=1)], axis=1)
    t = jnp.arange(L)
    outs = []
    for g, w in enumerate(POOL_WINDOWS):
        lo = jnp.clip(t - w // 2, 0, L)
        hi = jnp.clip(t + w - w // 2, 0, L)
        cnt = (hi - lo).astype(jnp.float32)[None, :, None]
        outs.append((c[:, hi, g] - c[:, lo, g]) / cnt)
    pooled = (jnp.stack(outs, axis=2) - uf).astype(u.dtype)
    y = jnp.einsum('blgc,gcd->blgd', pooled, w_pool).reshape(B, L, POOL_DIM)
    return y * pool_scale


def even_mixer(h, w_in, sink, w_pool, pool_scale, w_out):
    B, L, _ = h.shape
    z = h @ w_in
    q, k, v, u = jnp.split(z, [Q_DIM, Q_DIM + KV_DIM, Q_DIM + 2 * KV_DIM], axis=-1)
    q = q.reshape(B, L, N_HEADS, HEAD_DIM)
    k = k.reshape(B, L, N_KV_HEADS, HEAD_DIM)
    v = v.reshape(B, L, N_KV_HEADS, HEAD_DIM)
    a = window_attention(q, k, v, sink)
    p = multiscale_pool(u, w_pool, pool_scale)
    return jnp.concatenate([a, p], axis=-1) @ w_out


def fourier_mixer(h, w_out):
    B, L, _ = h.shape
    hf = h.astype(jnp.float32).reshape(B, L, N_FOURIER_GROUPS, FOURIER_GROUP)
    f = jnp.fft.fft2(hf, axes=(1, 3), norm='ortho').real.astype(h.dtype).reshape(B, L, D_MODEL)
    return f @ w_out


def trunk(x, meta_tokens, norm_ffn1, ffn1_w_gate, ffn1_w_up, ffn1_w_down, norm_mix,
          w_in_ap, attn_sink, pool_w, pool_scale, w_out_ap, w_out_fourier,
          norm_ffn2, ffn2_w_gate, ffn2_w_up, ffn2_w_down, norm_final):
    B = x.shape[0]
    meta = jnp.broadcast_to(meta_tokens[None].astype(x.dtype), (B, N_META, D_MODEL))
    h = jnp.concatenate([meta, x], axis=1)
    for layer in range(DEPTH):
        h = h + 0.5 * swiglu(rms_norm(h, norm_ffn1[layer]), ffn1_w_gate[layer],
                             ffn1_w_up[layer], ffn1_w_down[layer])
        hn = rms_norm(h, norm_mix[layer])
        if layer % 2 == 0:
            i = layer // 2
            h = h + even_mixer(hn, w_in_ap[i], attn_sink[i], pool_w[i], pool_scale[i], w_out_ap[i])
        else:
            h = h + fourier_mixer(hn, w_out_fourier[layer // 2])
        h = h + 0.5 * swiglu(rms_norm(h, norm_ffn2[layer]), ffn2_w_gate[layer],
                             ffn2_w_up[layer], ffn2_w_down[layer])
    return rms_norm(h, norm_final)[:, N_META:]


def setup_inputs(seed: int = 0) -> dict:
    key = jax.random.key(seed)
    ks = jax.random.split(key, 20)
    f32 = jnp.float32

    def nrm(k, shape, scale):
        return jax.random.normal(k, shape, f32) * scale

    def gain(k, shape):
        return 1.0 + 0.02 * jax.random.normal(k, shape, f32)

    return {
        "x_prompt": nrm(ks[0], (BATCH, SEQ, D_MODEL), 1.0),
        "x_sample": nrm(ks[1], (DEC_BATCH, DEC_SEQ, D_MODEL), 1.0),
        "meta_tokens": nrm(ks[2], (N_META, D_MODEL), 1.0),
        "norm_ffn1": gain(ks[3], (DEPTH, D_MODEL)),
        "ffn1_w_gate": nrm(ks[4], (DEPTH, D_MODEL, D_FF), D_MODEL ** -0.5),
        "ffn1_w_up": nrm(ks[5], (DEPTH, D_MODEL, D_FF), D_MODEL ** -0.5),
        "ffn1_w_down": nrm(ks[6], (DEPTH, D_FF, D_MODEL), D_FF ** -0.5),
        "norm_mix": gain(ks[7], (DEPTH, D_MODEL)),
        "w_in_ap": nrm(ks[8], (N_EVEN, D_MODEL, IN_DIM), D_MODEL ** -0.5),
        "attn_sink": nrm(ks[9], (N_EVEN, N_HEADS), 0.5),
        "pool_w": nrm(ks[10], (N_EVEN, N_POOL_GROUPS, POOL_GROUP, POOL_GROUP), POOL_GROUP ** -0.5),
        "pool_scale": gain(ks[11], (N_EVEN, POOL_DIM)),
        "w_out_ap": nrm(ks[12], (N_EVEN, MIX_DIM, D_MODEL), MIX_DIM ** -0.5),
        "w_out_fourier": nrm(ks[13], (N_ODD, D_MODEL, D_MODEL), D_MODEL ** -0.5),
        "norm_ffn2": gain(ks[14], (DEPTH, D_MODEL)),
        "ffn2_w_gate": nrm(ks[15], (DEPTH, D_MODEL, D_FF), D_MODEL ** -0.5),
        "ffn2_w_up": nrm(ks[16], (DEPTH, D_MODEL, D_FF), D_MODEL ** -0.5),
        "ffn2_w_down": nrm(ks[17], (DEPTH, D_FF, D_MODEL), D_FF ** -0.5),
        "norm_final": gain(ks[18], (D_MODEL,)),
    }


def reference(x_prompt, x_sample, meta_tokens, norm_ffn1, ffn1_w_gate, ffn1_w_up, ffn1_w_down,
              norm_mix, w_in_ap, attn_sink, pool_w, pool_scale, w_out_ap, w_out_fourier,
              norm_ffn2, ffn2_w_gate, ffn2_w_up, ffn2_w_down, norm_final):
    y_prompt = trunk(x_prompt, meta_tokens, norm_ffn1, ffn1_w_gate, ffn1_w_up, ffn1_w_down,
                     norm_mix, w_in_ap, attn_sink, pool_w, pool_scale, w_out_ap, w_out_fourier,
                     norm_ffn2, ffn2_w_gate, ffn2_w_up, ffn2_w_down, norm_final)
    y_sample = trunk(x_sample, meta_tokens, norm_ffn1, ffn1_w_gate, ffn1_w_up, ffn1_w_down,
                     norm_mix, w_in_ap, attn_sink, pool_w, pool_scale, w_out_ap, w_out_fourier,
                     norm_ffn2, ffn2_w_gate, ffn2_w_up, ffn2_w_down, norm_final)
    return (y_prompt, y_sample)
```

```python
import functools

import numpy as np
import jax
import jax.numpy as jnp
from jax import lax
from jax.experimental import pallas as pl
from jax.experimental.pallas import tpu as pltpu

D_MODEL = 1024
N_META = 16
BLOCK = 128
WINDOW = 128
N_HEADS = 8
N_KV_HEADS = 2
GQA_GROUP = N_HEADS // N_KV_HEADS
HEAD_DIM = 64
Q_DIM = N_HEADS * HEAD_DIM
KV_DIM = N_KV_HEADS * HEAD_DIM
POOL_DIM = D_MODEL // 2
POOL_WINDOWS = (2, 4, 8, 16)
POOL_GROUP = POOL_DIM // len(POOL_WINDOWS)
POOL_HALO = 8
N_FOURIER_GROUPS = 4
FOURIER_GROUP = D_MODEL // N_FOURIER_GROUPS
D_FF = 2816
EPS = 1e-6
NEG_INF = -1e30

F32 = jnp.float32
BF16 = jnp.bfloat16

V7X_VMEM_BYTES = 64 * 1024 * 1024
VMEM_LIMIT_BYTES = V7X_VMEM_BYTES - 8 * 1024 * 1024

FFN_ROWS = 512
PROJ_ROWS = 512
MIX_ROWS = 256
DFT_ROWS = 512


def _params(n_axes):
    return pltpu.CompilerParams(
        dimension_semantics=("arbitrary",) * n_axes,
        vmem_limit_bytes=VMEM_LIMIT_BYTES,
    )


def _resident(shape):
    zeros = (0,) * len(shape)
    return pl.BlockSpec(shape, lambda *_: zeros, pipeline_mode=pl.Buffered(1))


def _rms(x, g):
    y = x * lax.rsqrt(jnp.mean(x * x, axis=-1, keepdims=True) + EPS)
    return y * g


def _dot(a, b):
    return jnp.dot(a, b, preferred_element_type=F32)


def _ffn_kernel(x_ref, g_ref, wg_ref, wu_ref, wd_ref, *rest, final_norm):
    if final_norm:
        gf_ref, o_ref = rest
    else:
        (o_ref,) = rest
    x = x_ref[...]
    xn = _rms(x, g_ref[...]).astype(BF16)
    gate = _dot(xn, wg_ref[...])
    up = _dot(xn, wu_ref[...])
    act = (gate * jax.nn.sigmoid(gate) * up).astype(BF16)
    y = x + 0.5 * _dot(act, wd_ref[...])
    if final_norm:
        y = _rms(y, gf_ref[...])
    o_ref[...] = y


def _ffn(x2d, g, wg, wu, wd, g_final=None):
    n = x2d.shape[0]
    rows = min(FFN_ROWS, n)
    final_norm = g_final is not None
    row_spec = pl.BlockSpec((rows, D_MODEL), lambda i: (i, 0))
    in_specs = [row_spec, _resident((1, D_MODEL)), _resident((D_MODEL, D_FF)),
                _resident((D_MODEL, D_FF)), _resident((D_FF, D_MODEL))]
    args = [x2d, g.reshape(1, D_MODEL), wg, wu, wd]
    if final_norm:
        in_specs.append(_resident((1, D_MODEL)))
        args.append(g_final.reshape(1, D_MODEL))
    return pl.pallas_call(
        functools.partial(_ffn_kernel, final_norm=final_norm),
        out_shape=jax.ShapeDtypeStruct((n, D_MODEL), F32),
        grid=(pl.cdiv(n, rows),),
        in_specs=in_specs,
        out_specs=row_spec,
        compiler_params=_params(1),
        name="ffn_final" if final_norm else "ffn",
    )(*args)


PROJ_COLS = Q_DIM + 4 * KV_DIM + POOL_DIM


def _proj_kernel(h_ref, g_ref, w_ref, q_ref, kv_ref, u_ref):
    hn = _rms(h_ref[0], g_ref[...]).astype(BF16)
    z = _dot(hn, w_ref[...])
    q_ref[0] = z[:, :Q_DIM].astype(BF16)
    kv_ref[0] = z[:, Q_DIM:Q_DIM + 4 * KV_DIM].astype(BF16)
    u_ref[0] = z[:, Q_DIM + 4 * KV_DIM:]


def _proj(h3d, g, w_ext):
    b, s, _ = h3d.shape
    rows = min(PROJ_ROWS, s)
    spec = lambda width: pl.BlockSpec((1, rows, width), lambda bi, i: (bi, i, 0))
    return pl.pallas_call(
        _proj_kernel,
        out_shape=(jax.ShapeDtypeStruct((b, s, Q_DIM), BF16),
                   jax.ShapeDtypeStruct((b, s, 4 * KV_DIM), BF16),
                   jax.ShapeDtypeStruct((b, s, POOL_DIM), F32)),
        grid=(b, s // rows),
        in_specs=[spec(D_MODEL), _resident((1, D_MODEL)), _resident((D_MODEL, PROJ_COLS))],
        out_specs=(spec(Q_DIM), spec(4 * KV_DIM), spec(POOL_DIM)),
        compiler_params=_params(2),
        name="mix_proj",
    )(h3d, g.reshape(1, D_MODEL), w_ext)


def _window_sums(u_ext, rows):
    n = u_ext.shape[0]

    def sh(x, k):
        return pltpu.roll(x, k % n, axis=0)

    outs = []
    for gi, w in enumerate(POOL_WINDOWS):
        x = u_ext[:, gi * POOL_GROUP:(gi + 1) * POOL_GROUP]
        acc = x + sh(x, 1)
        half = 1
        while 2 * half < w:
            acc = sh(acc, half) + sh(acc, -half)
            half *= 2
        outs.append(acc[POOL_HALO:POOL_HALO + rows])
    return outs


def _mix_kernel(sink_ref, h_ref, q_ref, kvc_ref, kvp_ref, kvt_ref, kvn_ref,
                uc_ref, up_ref, ut_ref, un_ref, pw_ref, ps_ref, wo_ref, o_ref,
                *, rows, tail_query):
    nblk = rows // BLOCK
    i = pl.program_id(1)
    if tail_query:
        prev_from, cur_from, next_ok = BLOCK, BLOCK - N_META, True
        kv_prev = kvc_ref[0, :BLOCK]
        u_prev = jnp.zeros((POOL_HALO, POOL_DIM), F32)
        u_next = un_ref[0, :POOL_HALO]
        first_local = POOL_HALO + cur_from
        end_local = rows + 2 * POOL_HALO
    else:
        first = i == 0
        last = i == pl.num_programs(1) - 1
        prev_from = jnp.where(first, BLOCK - N_META, 0)
        cur_from, next_ok = 0, jnp.logical_not(last)
        kv_prev = jnp.where(first, kvt_ref[0], kvp_ref[0])
        u_prev = jnp.where(first, ut_ref[0, BLOCK - POOL_HALO:], up_ref[0, BLOCK - POOL_HALO:])
        u_next = jnp.where(last, 0.0, un_ref[0, :POOL_HALO])
        first_local = 0
        end_local = jnp.where(last, rows + POOL_HALO, rows + 2 * POOL_HALO)

    kv_all = jnp.concatenate([kv_prev, kvc_ref[0], kvn_ref[0]], axis=0)
    lane = lax.broadcasted_iota(jnp.int32, (1, BLOCK), 1)
    lo_half = lane < HEAD_DIM
    r_all = lax.broadcasted_iota(jnp.int32, (1, rows + 2 * BLOCK), 1)
    key_lo = jnp.where(prev_from < BLOCK, prev_from, BLOCK + cur_from)
    key_hi = jnp.where(next_ok, rows + 2 * BLOCK, rows + BLOCK)
    key_mask = jnp.where((r_all >= key_lo) & (r_all < key_hi), 0.0, NEG_INF).astype(F32)

    qi = lax.broadcasted_iota(jnp.int32, (BLOCK, 3 * BLOCK), 0)
    ki = lax.broadcasted_iota(jnp.int32, (BLOCK, 3 * BLOCK), 1)
    dist_i = jnp.abs(qi + BLOCK - ki)
    dist = dist_i.astype(F32)
    bias = [jnp.where(dist_i <= WINDOW, -(2.0 ** (-8.0 * (hd + 1) / N_HEADS)) * dist, NEG_INF)
            for hd in range(N_HEADS)]

    q_all = q_ref[0]
    zero = jnp.zeros((), BF16)
    attn_blocks = []
    for n in range(nblk):
        cm = key_mask[:, n * BLOCK:(n + 3) * BLOCK]
        pair_outs = []
        for kh in range(N_KV_HEADS):
            kd = kv_all[n * BLOCK:(n + 3) * BLOCK, kh * 2 * HEAD_DIM:(kh + 1) * 2 * HEAD_DIM]
            v_off = 2 * KV_DIM + kh * 2 * HEAD_DIM
            vd = kv_all[n * BLOCK:(n + 3) * BLOCK, v_off:v_off + 2 * HEAD_DIM]
            v_bd = jnp.concatenate([jnp.where(lo_half, vd, zero),
                                    jnp.where(lo_half, zero, vd)], axis=0)
            for pr in range(GQA_GROUP // 2):
                head0 = kh * GQA_GROUP + 2 * pr
                qp = q_all[n * BLOCK:(n + 1) * BLOCK, head0 * HEAD_DIM:(head0 + 2) * HEAD_DIM]
                ps, inv = [], []
                for side in range(2):
                    hd = head0 + side
                    qm = jnp.where(lo_half, qp, zero) if side == 0 else jnp.where(lo_half, zero, qp)
                    s = lax.dot_general(qm, kd, (((1,), (1,)), ((), ())),
                                        preferred_element_type=F32)
                    s = s + bias[hd] + cm
                    sk = sink_ref[hd]
                    m = jnp.maximum(jnp.max(s, axis=-1, keepdims=True), sk)
                    p = jnp.exp(s - m)
                    denom = jnp.sum(p, axis=-1, keepdims=True) + jnp.exp(sk - m)
                    ps.append(p.astype(BF16))
                    inv.append(1.0 / denom)
                o = _dot(jnp.concatenate(ps, axis=1), v_bd)
                pair_outs.append(o * jnp.where(lo_half, inv[0], inv[1]))
        attn_blocks.append(jnp.concatenate(pair_outs, axis=1))
    attn = jnp.concatenate(attn_blocks, axis=0).astype(BF16)

    u_cur = uc_ref[0]
    u_ext = jnp.concatenate([u_prev, u_cur, u_next], axis=0)
    sums = _window_sums(u_ext, rows)
    j = lax.broadcasted_iota(jnp.int32, (rows, 1), 0) + POOL_HALO
    pooled = []
    for gi, w in enumerate(POOL_WINDOWS):
        cnt = jnp.minimum(j + (w - w // 2), end_local) - jnp.maximum(j - w // 2, first_local)
        cnt = jnp.maximum(cnt, 1).astype(F32)
        x = u_cur[:, gi * POOL_GROUP:(gi + 1) * POOL_GROUP]
        pg = (sums[gi] / cnt - x).astype(BF16)
        pooled.append(_dot(pg, pw_ref[gi]))
    y_pool = (jnp.concatenate(pooled, axis=1) * ps_ref[...]).astype(BF16)

    out = h_ref[0] + _dot(attn, wo_ref[:Q_DIM]) + _dot(y_pool, wo_ref[Q_DIM:])
    o_ref[0] = out


def _mix(h, q, kv, u, q_t, kv_t, u_t, sink, pool_w, pool_scale, w_out, *, tail_query):
    if tail_query:
        hq, qq, kvq, uq = h, q_t, kv_t, u_t
    else:
        hq, qq, kvq, uq = h, q, kv, u
    b, s, _ = hq.shape
    rows = min(MIX_ROWS, s)
    rb = rows // BLOCK
    nb_main = kv.shape[1] // BLOCK
    cur = lambda width: pl.BlockSpec((1, rows, width), lambda bi, i: (bi, i, 0))
    blk = lambda width, fn: pl.BlockSpec((1, BLOCK, width), fn)
    if tail_query:
        prev_map = lambda bi, i: (bi, 0, 0)
        next_map = lambda bi, i: (bi, 0, 0)
    else:
        prev_map = lambda bi, i: (bi, jnp.maximum(i * rb - 1, 0), 0)
        next_map = lambda bi, i: (bi, jnp.minimum((i + 1) * rb, nb_main - 1), 0)
    tail_map = lambda bi, i: (bi, 0, 0)
    kvw = 4 * KV_DIM
    in_specs = [
        pl.BlockSpec(memory_space=pltpu.SMEM),
        cur(D_MODEL), cur(Q_DIM),
        cur(kvw), blk(kvw, prev_map), blk(kvw, tail_map), blk(kvw, next_map),
        cur(POOL_DIM), blk(POOL_DIM, prev_map), blk(POOL_DIM, tail_map), blk(POOL_DIM, next_map),
        _resident((len(POOL_WINDOWS), POOL_GROUP, POOL_GROUP)),
        _resident((1, POOL_DIM)),
        _resident((Q_DIM + POOL_DIM, D_MODEL)),
    ]
    return pl.pallas_call(
        functools.partial(_mix_kernel, rows=rows, tail_query=tail_query),
        out_shape=jax.ShapeDtypeStruct((b, s, D_MODEL), F32),
        grid=(b, s // rows),
        in_specs=in_specs,
        out_specs=cur(D_MODEL),
        compiler_params=_params(2),
        name="mix_tail" if tail_query else "mix_main",
    )(sink, hq, qq, kvq, kv, kv_t, kv, uq, u, u_t, u, pool_w, pool_scale.reshape(1, POOL_DIM), w_out)


def _chan_dft_kernel(h_ref, g_ref, cs_ref, o_ref):
    hn = _rms(h_ref[0], g_ref[...]).astype(BF16)
    cs = cs_ref[...]
    for gi in range(N_FOURIER_GROUPS):
        lo = gi * FOURIER_GROUP
        r = _dot(hn[:, lo:lo + FOURIER_GROUP], cs)
        o_ref[0, :, lo:lo + FOURIER_GROUP] = r[:, :FOURIER_GROUP].astype(BF16)
        o_ref[0, :, D_MODEL + lo:D_MODEL + lo + FOURIER_GROUP] = r[:, FOURIER_GROUP:].astype(BF16)


def _chan_dft(h3d, g, cs):
    b, s, _ = h3d.shape
    rows = min(PROJ_ROWS, s)
    return pl.pallas_call(
        _chan_dft_kernel,
        out_shape=jax.ShapeDtypeStruct((b, s, 2 * D_MODEL), BF16),
        grid=(b, s // rows),
        in_specs=[pl.BlockSpec((1, rows, D_MODEL), lambda bi, i: (bi, i, 0)),
                  _resident((1, D_MODEL)), _resident((FOURIER_GROUP, 2 * FOURIER_GROUP))],
        out_specs=pl.BlockSpec((1, rows, 2 * D_MODEL), lambda bi, i: (bi, i, 0)),
        compiler_params=_params(2),
        name="chan_dft",
    )(h3d, g.reshape(1, D_MODEL), cs)


def _twiddle_kernel(ca_ref, sa_ref, cb_ref, sb_ref, wc_ref, ws_ref):
    ca, sa = ca_ref[0], sa_ref[0]
    cb, sb = cb_ref[...], sb_ref[...]
    wc_ref[...] = (ca * cb - sa * sb).astype(BF16)
    ws_ref[...] = (-(sa * cb + ca * sb)).astype(BF16)


def _twiddles(s):
    l_seq = s + N_META
    lk = s + BLOCK
    nblk = lk // BLOCK
    idx = np.arange(lk, dtype=np.int64)
    pos = np.where(idx < s, idx + N_META, idx - s)
    col_ok = (idx < s + N_META).astype(np.float64)
    row0 = np.where(np.arange(nblk) < s // BLOCK, N_META + BLOCK * np.arange(nblk), 0).astype(np.int64)
    theta = 2.0 * np.pi / l_seq
    alpha = theta * ((row0[:, None] * pos[None, :]) % l_seq)
    beta = theta * ((np.arange(BLOCK, dtype=np.int64)[:, None] * pos[None, :]) % l_seq)
    scale = col_ok / np.sqrt(float(l_seq) * FOURIER_GROUP)
    ca = jnp.asarray((np.cos(alpha) * scale).reshape(nblk, 1, lk), F32)
    sa = jnp.asarray((np.sin(alpha) * scale).reshape(nblk, 1, lk), F32)
    cb = jnp.asarray(np.cos(beta), F32)
    sb = jnp.asarray(np.sin(beta), F32)
    row_spec = pl.BlockSpec((1, 1, lk), lambda i: (i, 0, 0))
    out_spec = pl.BlockSpec((BLOCK, lk), lambda i: (i, 0))
    return pl.pallas_call(
        _twiddle_kernel,
        out_shape=(jax.ShapeDtypeStruct((lk, lk), BF16),) * 2,
        grid=(nblk,),
        in_specs=[row_spec, row_spec, _resident((BLOCK, lk)), _resident((BLOCK, lk))],
        out_specs=(out_spec, out_spec),
        compiler_params=_params(1),
        name="twiddles",
    )(ca, sa, cb, sb)


def _seq_dft_kernel(wc_ref, ws_ref, gm_ref, gt_ref, h_ref, wo_ref, o_ref, *, s):
    a_m, b_m = gm_ref[0, :, :D_MODEL], gm_ref[0, :, D_MODEL:]
    a_t, b_t = gt_ref[0, :, :D_MODEL], gt_ref[0, :, D_MODEL:]
    y = (_dot(wc_ref[:, :s], a_m) + _dot(ws_ref[:, :s], b_m)
         + _dot(wc_ref[:, s:], a_t) + _dot(ws_ref[:, s:], b_t))
    o_ref[0] = h_ref[0] + _dot(y.astype(BF16), wo_ref[...])


def _seq_dft(wc, ws, g_main, g_tail, h, w_out):
    b, s, _ = g_main.shape
    lk = s + BLOCK
    rows = DFT_ROWS
    w_spec = pl.BlockSpec((rows, lk), lambda bi, i: (i, 0))
    h_spec = pl.BlockSpec((1, rows, D_MODEL), lambda bi, i: (bi, i, 0))
    return pl.pallas_call(
        functools.partial(_seq_dft_kernel, s=s),
        out_shape=jax.ShapeDtypeStruct((b, s, D_MODEL), F32),
        grid=(b, s // rows),
        in_specs=[w_spec, w_spec,
                  pl.BlockSpec((1, s, 2 * D_MODEL), lambda bi, i: (bi, 0, 0),
                               pipeline_mode=pl.Buffered(1)),
                  pl.BlockSpec((1, BLOCK, 2 * D_MODEL), lambda bi, i: (bi, 0, 0)),
                  h_spec, _resident((D_MODEL, D_MODEL))],
        out_specs=h_spec,
        compiler_params=_params(2),
        name="seq_dft",
    )(wc, ws, g_main, g_tail, h, w_out)


def _prep_weights(p):
    bf = lambda a: a.astype(BF16)
    w_in = p["w_in_ap"][0]
    wq = w_in[:, :Q_DIM] * (HEAD_DIM ** -0.5)
    wk = w_in[:, Q_DIM:Q_DIM + KV_DIM].reshape(D_MODEL, N_KV_HEADS, 1, HEAD_DIM)
    wv = w_in[:, Q_DIM + KV_DIM:Q_DIM + 2 * KV_DIM].reshape(D_MODEL, N_KV_HEADS, 1, HEAD_DIM)
    dup = lambda w: jnp.broadcast_to(w, (D_MODEL, N_KV_HEADS, 2, HEAD_DIM)).reshape(D_MODEL, 2 * KV_DIM)
    w_ext = jnp.concatenate([wq, dup(wk), dup(wv), w_in[:, Q_DIM + 2 * KV_DIM:]], axis=1)
    k = np.arange(FOURIER_GROUP, dtype=np.int64)
    ang = 2.0 * np.pi * ((k[:, None] * k[None, :]) % FOURIER_GROUP) / FOURIER_GROUP
    cs = jnp.asarray(np.concatenate([np.cos(ang), np.sin(ang)], axis=1), F32)
    return dict(
        ffn1=[(p["norm_ffn1"][l], bf(p["ffn1_w_gate"][l]), bf(p["ffn1_w_up"][l]), bf(p["ffn1_w_down"][l]))
              for l in range(2)],
        ffn2=[(p["norm_ffn2"][l], bf(p["ffn2_w_gate"][l]), bf(p["ffn2_w_up"][l]), bf(p["ffn2_w_down"][l]))
              for l in range(2)],
        norm_mix=p["norm_mix"],
        w_ext=bf(w_ext), sink=p["attn_sink"][0], pool_w=bf(p["pool_w"][0]),
        pool_scale=p["pool_scale"][0], w_out_ap=bf(p["w_out_ap"][0]),
        cs=bf(cs), w_out_f=bf(p["w_out_fourier"][0]), norm_final=p["norm_final"],
    )


def _trunk(x, meta, w):
    b, s, _ = x.shape
    flat = lambda a: a.reshape(-1, D_MODEL)
    hm = jnp.broadcast_to(meta[None], (b, N_META, D_MODEL))
    front_pad = lambda a: jnp.pad(a, ((0, 0), (BLOCK - N_META, 0), (0, 0)))
    back_pad = lambda a: jnp.pad(a, ((0, 0), (0, BLOCK - N_META), (0, 0)))

    h = _ffn(flat(x), *w["ffn1"][0]).reshape(b, s, D_MODEL)
    hm = _ffn(flat(hm), *w["ffn1"][0]).reshape(b, N_META, D_MODEL)
    hm_t = front_pad(hm)
    q, kv, u = _proj(h, w["norm_mix"][0], w["w_ext"])
    q_t, kv_t, u_t = _proj(hm_t, w["norm_mix"][0], w["w_ext"])
    mix_args = (w["sink"], w["pool_w"], w["pool_scale"], w["w_out_ap"])
    h_new = _mix(h, q, kv, u, q_t, kv_t, u_t, *mix_args, tail_query=False)
    hm = _mix(hm_t, q, kv, u, q_t, kv_t, u_t, *mix_args, tail_query=True)[:, BLOCK - N_META:]
    h = _ffn(flat(h_new), *w["ffn2"][0]).reshape(b, s, D_MODEL)
    hm = _ffn(flat(hm), *w["ffn2"][0]).reshape(b, N_META, D_MODEL)

    h = _ffn(flat(h), *w["ffn1"][1]).reshape(b, s, D_MODEL)
    hm = _ffn(flat(hm), *w["ffn1"][1]).reshape(b, N_META, D_MODEL)
    hm_t = back_pad(hm)
    g_main = _chan_dft(h, w["norm_mix"][1], w["cs"])
    g_tail = _chan_dft(hm_t, w["norm_mix"][1], w["cs"])
    wc, ws = _twiddles(s)
    h = _seq_dft(wc, ws, g_main, g_tail, h, w["w_out_f"])
    return _ffn(flat(h), *w["ffn2"][1], g_final=w["norm_final"]).reshape(b, s, D_MODEL)


def kernel(x_prompt, x_sample, meta_tokens, norm_ffn1, ffn1_w_gate, ffn1_w_up, ffn1_w_down, norm_mix, w_in_ap, attn_sink, pool_w, pool_scale, w_out_ap, w_out_fourier, norm_ffn2, ffn2_w_gate, ffn2_w_up, ffn2_w_down, norm_final):
    w = _prep_weights(dict(
        norm_ffn1=norm_ffn1, ffn1_w_gate=ffn1_w_gate, ffn1_w_up=ffn1_w_up, ffn1_w_down=ffn1_w_down,
        norm_mix=norm_mix, w_in_ap=w_in_ap, attn_sink=attn_sink, pool_w=pool_w, pool_scale=pool_scale,
        w_out_ap=w_out_ap, w_out_fourier=w_out_fourier, norm_ffn2=norm_ffn2, ffn2_w_gate=ffn2_w_gate,
        ffn2_w_up=ffn2_w_up, ffn2_w_down=ffn2_w_down, norm_final=norm_final))
    return (_trunk(x_prompt, meta_tokens, w), _trunk(x_sample, meta_tokens, w))
```

```python
import functools

import numpy as np
import jax
import jax.numpy as jnp
from jax import lax
from jax.experimental import pallas as pl
from jax.experimental.pallas import tpu as pltpu

D_MODEL = 1024
N_META = 16
BLOCK = 128
WINDOW = 128
N_HEADS = 8
N_KV_HEADS = 2
GQA_GROUP = N_HEADS // N_KV_HEADS
HEAD_DIM = 64
Q_DIM = N_HEADS * HEAD_DIM
KV_DIM = N_KV_HEADS * HEAD_DIM
POOL_DIM = D_MODEL // 2
POOL_WINDOWS = (2, 4, 8, 16)
POOL_GROUP = POOL_DIM // len(POOL_WINDOWS)
POOL_HALO = 8
N_FOURIER_GROUPS = 4
FOURIER_GROUP = D_MODEL // N_FOURIER_GROUPS
D_FF = 2816
EPS = 1e-6
NEG_INF = -1e30

F32 = jnp.float32
BF16 = jnp.bfloat16

V7X_VMEM_BYTES = 64 * 1024 * 1024
VMEM_LIMIT_BYTES = V7X_VMEM_BYTES - 8 * 1024 * 1024

FFN_ROWS = 512
PROJ_ROWS = 512
MIX_ROWS = 256
DFT_ROWS = 512
FOLD_ROWS = 384


def _params(n_axes):
    return pltpu.CompilerParams(
        dimension_semantics=("arbitrary",) * n_axes,
        vmem_limit_bytes=VMEM_LIMIT_BYTES,
    )


def _resident(shape):
    zeros = (0,) * len(shape)
    return pl.BlockSpec(shape, lambda *_: zeros, pipeline_mode=pl.Buffered(1))


def _rms(x, g):
    y = x * lax.rsqrt(jnp.mean(x * x, axis=-1, keepdims=True) + EPS)
    return y * g


def _dot(a, b):
    return jnp.dot(a, b, preferred_element_type=F32)


def _ffn_kernel(x_ref, g_ref, wg_ref, wu_ref, wd_ref, *rest, post):
    x = x_ref[...]
    xn = _rms(x, g_ref[...]).astype(BF16)
    gate = _dot(xn, wg_ref[...])
    up = _dot(xn, wu_ref[...])
    act = (gate * jax.nn.sigmoid(gate) * up).astype(BF16)
    y = x + 0.5 * _dot(act, wd_ref[...])
    if post is None:
        (o_ref,) = rest
        o_ref[...] = y
    elif post == "final":
        gp_ref, o_ref = rest
        o_ref[...] = _rms(y, gp_ref[...])
    else:
        gp_ref, o_ref, on_ref = rest
        o_ref[...] = y
        on_ref[...] = _rms(y, gp_ref[...]).astype(BF16)


def _ffn(x2d, g, wg, wu, wd, g_post=None, post=None):
    n = x2d.shape[0]
    rows = min(FFN_ROWS, n)
    row_spec = pl.BlockSpec((rows, D_MODEL), lambda i: (i, 0))
    in_specs = [row_spec, _resident((1, D_MODEL)), _resident((D_MODEL, D_FF)),
                _resident((D_MODEL, D_FF)), _resident((D_FF, D_MODEL))]
    args = [x2d, g.reshape(1, D_MODEL), wg, wu, wd]
    out_shape = jax.ShapeDtypeStruct((n, D_MODEL), F32)
    out_specs = row_spec
    if post is not None:
        in_specs.append(_resident((1, D_MODEL)))
        args.append(g_post.reshape(1, D_MODEL))
    if post == "normed":
        out_shape = (out_shape, jax.ShapeDtypeStruct((n, D_MODEL), BF16))
        out_specs = (row_spec, row_spec)
    return pl.pallas_call(
        functools.partial(_ffn_kernel, post=post),
        out_shape=out_shape,
        grid=(pl.cdiv(n, rows),),
        in_specs=in_specs,
        out_specs=out_specs,
        compiler_params=_params(1),
        name="ffn" if post is None else "ffn_" + post,
    )(*args)


PROJ_COLS = Q_DIM + 4 * KV_DIM + POOL_DIM


def _proj_kernel(h_ref, g_ref, w_ref, q_ref, kv_ref, u_ref):
    hn = _rms(h_ref[0], g_ref[...]).astype(BF16)
    z = _dot(hn, w_ref[...])
    q_ref[0] = z[:, :Q_DIM].astype(BF16)
    kv_ref[0] = z[:, Q_DIM:Q_DIM + 4 * KV_DIM].astype(BF16)
    u_ref[0] = z[:, Q_DIM + 4 * KV_DIM:]


def _proj(h3d, g, w_ext):
    b, s, _ = h3d.shape
    rows = min(PROJ_ROWS, s)
    spec = lambda width: pl.BlockSpec((1, rows, width), lambda bi, i: (bi, i, 0))
    return pl.pallas_call(
        _proj_kernel,
        out_shape=(jax.ShapeDtypeStruct((b, s, Q_DIM), BF16),
                   jax.ShapeDtypeStruct((b, s, 4 * KV_DIM), BF16),
                   jax.ShapeDtypeStruct((b, s, POOL_DIM), F32)),
        grid=(b, s // rows),
        in_specs=[spec(D_MODEL), _resident((1, D_MODEL)), _resident((D_MODEL, PROJ_COLS))],
        out_specs=(spec(Q_DIM), spec(4 * KV_DIM), spec(POOL_DIM)),
        compiler_params=_params(2),
        name="mix_proj",
    )(h3d, g.reshape(1, D_MODEL), w_ext)


def _window_sums(u_ext, rows):
    n = u_ext.shape[0]

    def sh(x, k):
        return pltpu.roll(x, k % n, axis=0)

    outs = []
    for gi, w in enumerate(POOL_WINDOWS):
        x = u_ext[:, gi * POOL_GROUP:(gi + 1) * POOL_GROUP]
        acc = x + sh(x, 1)
        half = 1
        while 2 * half < w:
            acc = sh(acc, half) + sh(acc, -half)
            half *= 2
        outs.append(acc[POOL_HALO:POOL_HALO + rows])
    return outs


def _mix_kernel(sink_ref, h_ref, q_ref, kvc_ref, kvp_ref, kvt_ref, kvn_ref,
                uc_ref, up_ref, ut_ref, un_ref, pw_ref, ps_ref, wo_ref, o_ref,
                *, rows, tail_query):
    nblk = rows // BLOCK
    i = pl.program_id(1)
    if tail_query:
        prev_from, cur_from, next_ok = BLOCK, BLOCK - N_META, True
        kv_prev = kvc_ref[0, :BLOCK]
        u_prev = jnp.zeros((POOL_HALO, POOL_DIM), F32)
        u_next = un_ref[0, :POOL_HALO]
        first_local = POOL_HALO + cur_from
        end_local = rows + 2 * POOL_HALO
    else:
        first = i == 0
        last = i == pl.num_programs(1) - 1
        prev_from = jnp.where(first, BLOCK - N_META, 0)
        cur_from, next_ok = 0, jnp.logical_not(last)
        kv_prev = jnp.where(first, kvt_ref[0], kvp_ref[0])
        u_prev = jnp.where(first, ut_ref[0, BLOCK - POOL_HALO:], up_ref[0, BLOCK - POOL_HALO:])
        u_next = jnp.where(last, 0.0, un_ref[0, :POOL_HALO])
        first_local = 0
        end_local = jnp.where(last, rows + POOL_HALO, rows + 2 * POOL_HALO)

    kv_all = jnp.concatenate([kv_prev, kvc_ref[0], kvn_ref[0]], axis=0)
    lane = lax.broadcasted_iota(jnp.int32, (1, BLOCK), 1)
    lo_half = lane < HEAD_DIM
    r_all = lax.broadcasted_iota(jnp.int32, (1, rows + 2 * BLOCK), 1)
    key_lo = jnp.where(prev_from < BLOCK, prev_from, BLOCK + cur_from)
    key_hi = jnp.where(next_ok, rows + 2 * BLOCK, rows + BLOCK)
    key_mask = jnp.where((r_all >= key_lo) & (r_all < key_hi), 0.0, NEG_INF).astype(F32)

    qi = lax.broadcasted_iota(jnp.int32, (BLOCK, 3 * BLOCK), 0)
    ki = lax.broadcasted_iota(jnp.int32, (BLOCK, 3 * BLOCK), 1)
    dist_i = jnp.abs(qi + BLOCK - ki)
    dist = dist_i.astype(F32)
    bias = [jnp.where(dist_i <= WINDOW, -(2.0 ** (-8.0 * (hd + 1) / N_HEADS)) * dist, NEG_INF)
            for hd in range(N_HEADS)]

    q_all = q_ref[0]
    zero = jnp.zeros((), BF16)
    attn_blocks = []
    for n in range(nblk):
        cm = key_mask[:, n * BLOCK:(n + 3) * BLOCK]
        pair_outs = []
        for kh in range(N_KV_HEADS):
            kd = kv_all[n * BLOCK:(n + 3) * BLOCK, kh * 2 * HEAD_DIM:(kh + 1) * 2 * HEAD_DIM]
            v_off = 2 * KV_DIM + kh * 2 * HEAD_DIM
            vd = kv_all[n * BLOCK:(n + 3) * BLOCK, v_off:v_off + 2 * HEAD_DIM]
            v_bd = jnp.concatenate([jnp.where(lo_half, vd, zero),
                                    jnp.where(lo_half, zero, vd)], axis=0)
            for pr in range(GQA_GROUP // 2):
                head0 = kh * GQA_GROUP + 2 * pr
                qp = q_all[n * BLOCK:(n + 1) * BLOCK, head0 * HEAD_DIM:(head0 + 2) * HEAD_DIM]
                ps, inv = [], []
                for side in range(2):
                    hd = head0 + side
                    qm = jnp.where(lo_half, qp, zero) if side == 0 else jnp.where(lo_half, zero, qp)
                    s = lax.dot_general(qm, kd, (((1,), (1,)), ((), ())),
                                        preferred_element_type=F32)
                    s = s + bias[hd] + cm
                    sk = sink_ref[hd]
                    m = jnp.maximum(jnp.max(s, axis=-1, keepdims=True), sk)
                    p = jnp.exp(s - m)
                    denom = jnp.sum(p, axis=-1, keepdims=True) + jnp.exp(sk - m)
                    ps.append(p.astype(BF16))
                    inv.append(1.0 / denom)
                o = _dot(jnp.concatenate(ps, axis=1), v_bd)
                pair_outs.append(o * jnp.where(lo_half, inv[0], inv[1]))
        attn_blocks.append(jnp.concatenate(pair_outs, axis=1))
    attn = jnp.concatenate(attn_blocks, axis=0).astype(BF16)

    u_cur = uc_ref[0]
    u_ext = jnp.concatenate([u_prev, u_cur, u_next], axis=0)
    sums = _window_sums(u_ext, rows)
    j = lax.broadcasted_iota(jnp.int32, (rows, 1), 0) + POOL_HALO
    pooled = []
    for gi, w in enumerate(POOL_WINDOWS):
        cnt = jnp.minimum(j + (w - w // 2), end_local) - jnp.maximum(j - w // 2, first_local)
        cnt = jnp.maximum(cnt, 1).astype(F32)
        x = u_cur[:, gi * POOL_GROUP:(gi + 1) * POOL_GROUP]
        pg = (sums[gi] / cnt - x).astype(BF16)
        pooled.append(_dot(pg, pw_ref[gi]))
    y_pool = (jnp.concatenate(pooled, axis=1) * ps_ref[...]).astype(BF16)

    out = h_ref[0] + _dot(attn, wo_ref[:Q_DIM]) + _dot(y_pool, wo_ref[Q_DIM:])
    o_ref[0] = out


def _mix(h, q, kv, u, q_t, kv_t, u_t, sink, pool_w, pool_scale, w_out, *, tail_query):
    if tail_query:
        hq, qq, kvq, uq = h, q_t, kv_t, u_t
    else:
        hq, qq, kvq, uq = h, q, kv, u
    b, s, _ = hq.shape
    rows = min(MIX_ROWS, s)
    rb = rows // BLOCK
    nb_main = kv.shape[1] // BLOCK
    cur = lambda width: pl.BlockSpec((1, rows, width), lambda bi, i: (bi, i, 0))
    blk = lambda width, fn: pl.BlockSpec((1, BLOCK, width), fn)
    if tail_query:
        prev_map = lambda bi, i: (bi, 0, 0)
        next_map = lambda bi, i: (bi, 0, 0)
    else:
        prev_map = lambda bi, i: (bi, jnp.maximum(i * rb - 1, 0), 0)
        next_map = lambda bi, i: (bi, jnp.minimum((i + 1) * rb, nb_main - 1), 0)
    tail_map = lambda bi, i: (bi, 0, 0)
    kvw = 4 * KV_DIM
    in_specs = [
        pl.BlockSpec(memory_space=pltpu.SMEM),
        cur(D_MODEL), cur(Q_DIM),
        cur(kvw), blk(kvw, prev_map), blk(kvw, tail_map), blk(kvw, next_map),
        cur(POOL_DIM), blk(POOL_DIM, prev_map), blk(POOL_DIM, tail_map), blk(POOL_DIM, next_map),
        _resident((len(POOL_WINDOWS), POOL_GROUP, POOL_GROUP)),
        _resident((1, POOL_DIM)),
        _resident((Q_DIM + POOL_DIM, D_MODEL)),
    ]
    return pl.pallas_call(
        functools.partial(_mix_kernel, rows=rows, tail_query=tail_query),
        out_shape=jax.ShapeDtypeStruct((b, s, D_MODEL), F32),
        grid=(b, s // rows),
        in_specs=in_specs,
        out_specs=cur(D_MODEL),
        compiler_params=_params(2),
        name="mix_tail" if tail_query else "mix_main",
    )(sink, hq, qq, kvq, kv, kv_t, kv, uq, u, u_t, u, pool_w, pool_scale.reshape(1, POOL_DIM), w_out)


FOLD_SUB = FOLD_ROWS // BLOCK


def _fold_len(s):
    n_first = (s + N_META) // 2 + 1 + (BLOCK - N_META)
    return pl.cdiv(n_first, FOLD_ROWS) * FOLD_ROWS


def _fold_kernel(*refs, s):
    first = refs[:FOLD_SUB]
    tail_ref = refs[FOLD_SUB]
    win = refs[FOLD_SUB + 1:2 * FOLD_SUB + 2]
    cs_ref, o_ref = refs[2 * FOLD_SUB + 2:]
    j = pl.program_id(1)
    is_first = j == 0

    ri = lax.broadcasted_iota(jnp.int32, (BLOCK, 2 * BLOCK), 0)
    ki = lax.broadcasted_iota(jnp.int32, (BLOCK, 2 * BLOCK), 1)
    rev = jnp.where(ki == 2 * BLOCK - N_META - ri, 1.0, 0.0).astype(BF16)

    zero = jnp.zeros((), BF16)
    w = [r[0] for r in win]
    w[FOLD_SUB] = jnp.where(is_first, zero, w[FOLD_SUB])
    half = s // 2 + BLOCK - N_META // 2
    cs = cs_ref[...]
    for q in range(FOLD_SUB):
        x1 = first[q][0]
        if q == 0:
            x1 = jnp.where(is_first, tail_ref[0], x1)
        pair = jnp.concatenate([w[FOLD_SUB - 1 - q], w[FOLD_SUB - q]], axis=0)
        partner = _dot(rev, pair)
        c = (j * FOLD_SUB + q) * BLOCK + lax.broadcasted_iota(jnp.int32, (BLOCK, 1), 0)
        x1 = jnp.where(c <= half, x1.astype(F32), 0.0)
        partner = jnp.where(c < half, partner, 0.0)
        hs = (x1 + partner).astype(BF16)
        hd = (x1 - partner).astype(BF16)
        rows = slice(q * BLOCK, (q + 1) * BLOCK)
        for gi in range(N_FOURIER_GROUPS):
            lo = gi * FOURIER_GROUP
            e = _dot(hs[:, lo:lo + FOURIER_GROUP], cs[:, :FOURIER_GROUP])
            o = _dot(hd[:, lo:lo + FOURIER_GROUP], cs[:, FOURIER_GROUP:])
            o_ref[0, rows, lo:lo + FOURIER_GROUP] = e.astype(BF16)
            o_ref[0, rows, D_MODEL + lo:D_MODEL + lo + FOURIER_GROUP] = o.astype(BF16)


def _fold(xn, xn_tail, cs):
    b, s, _ = xn.shape
    nb = s // BLOCK
    lf = _fold_len(s)
    blk = lambda fn: pl.BlockSpec((1, BLOCK, D_MODEL), fn)
    first_specs = [blk(lambda bi, j, q=q: (bi, jnp.maximum(j * FOLD_SUB + q - 1, 0), 0))
                   for q in range(FOLD_SUB)]
    win_specs = [blk(lambda bi, j, m=m: (bi, jnp.minimum(nb - FOLD_SUB * (j + 1) + m, nb - 1), 0))
                 for m in range(FOLD_SUB + 1)]
    return pl.pallas_call(
        functools.partial(_fold_kernel, s=s),
        out_shape=jax.ShapeDtypeStruct((b, lf, 2 * D_MODEL), BF16),
        grid=(b, lf // FOLD_ROWS),
        in_specs=first_specs + [blk(lambda bi, j: (bi, 0, 0))] + win_specs
                 + [_resident((FOURIER_GROUP, 2 * FOURIER_GROUP))],
        out_specs=pl.BlockSpec((1, FOLD_ROWS, 2 * D_MODEL), lambda bi, j: (bi, j, 0)),
        compiler_params=_params(2),
        name="fold_dft",
    )(*([xn] * FOLD_SUB), xn_tail, *([xn] * (FOLD_SUB + 1)), cs)


def _twiddle_kernel(ca_ref, sa_ref, cb_ref, sb_ref, wc_ref, ws_ref):
    ca, sa = ca_ref[0], sa_ref[0]
    cb, sb = cb_ref[...], sb_ref[...]
    wc_ref[...] = (ca * cb - sa * sb).astype(BF16)
    ws_ref[...] = (-(sa * cb + ca * sb)).astype(BF16)


def _twiddles(s):
    l_seq = s + N_META
    lf = _fold_len(s)
    nblk = s // BLOCK
    pos = np.arange(lf, dtype=np.int64) - (BLOCK - N_META)
    col_ok = ((pos >= 0) & (pos <= l_seq // 2)).astype(np.float64)
    pos = np.where(col_ok > 0, pos, 0)
    row0 = N_META + BLOCK * np.arange(nblk, dtype=np.int64)
    theta = 2.0 * np.pi / l_seq
    alpha = theta * ((row0[:, None] * pos[None, :]) % l_seq)
    beta = theta * ((np.arange(BLOCK, dtype=np.int64)[:, None] * pos[None, :]) % l_seq)
    scale = col_ok / np.sqrt(float(l_seq) * FOURIER_GROUP)
    ca = jnp.asarray((np.cos(alpha) * scale).reshape(nblk, 1, lf), F32)
    sa = jnp.asarray((np.sin(alpha) * scale).reshape(nblk, 1, lf), F32)
    cb = jnp.asarray(np.cos(beta), F32)
    sb = jnp.asarray(np.sin(beta), F32)
    row_spec = pl.BlockSpec((1, 1, lf), lambda i: (i, 0, 0))
    out_spec = pl.BlockSpec((BLOCK, lf), lambda i: (i, 0))
    return pl.pallas_call(
        _twiddle_kernel,
        out_shape=(jax.ShapeDtypeStruct((s, lf), BF16),) * 2,
        grid=(nblk,),
        in_specs=[row_spec, row_spec, _resident((BLOCK, lf)), _resident((BLOCK, lf))],
        out_specs=(out_spec, out_spec),
        compiler_params=_params(1),
        name="twiddles",
    )(ca, sa, cb, sb)


def _seq_dft_kernel(wc_ref, ws_ref, eo_ref, h_ref, wo_ref, o_ref):
    y = _dot(wc_ref[...], eo_ref[0, :, :D_MODEL]) + _dot(ws_ref[...], eo_ref[0, :, D_MODEL:])
    o_ref[0] = h_ref[0] + _dot(y.astype(BF16), wo_ref[...])


def _seq_dft(wc, ws, eo, h, w_out):
    b, s, _ = h.shape
    lf = eo.shape[1]
    rows = DFT_ROWS
    w_spec = pl.BlockSpec((rows, lf), lambda bi, i: (i, 0))
    h_spec = pl.BlockSpec((1, rows, D_MODEL), lambda bi, i: (bi, i, 0))
    return pl.pallas_call(
        _seq_dft_kernel,
        out_shape=jax.ShapeDtypeStruct((b, s, D_MODEL), F32),
        grid=(b, s // rows),
        in_specs=[w_spec, w_spec,
                  pl.BlockSpec((1, lf, 2 * D_MODEL), lambda bi, i: (bi, 0, 0)),
                  h_spec, _resident((D_MODEL, D_MODEL))],
        out_specs=h_spec,
        compiler_params=_params(2),
        name="seq_dft",
    )(wc, ws, eo, h, w_out)


def _prep_weights(p):
    bf = lambda a: a.astype(BF16)
    w_in = p["w_in_ap"][0]
    wq = w_in[:, :Q_DIM] * (HEAD_DIM ** -0.5)
    wk = w_in[:, Q_DIM:Q_DIM + KV_DIM].reshape(D_MODEL, N_KV_HEADS, 1, HEAD_DIM)
    wv = w_in[:, Q_DIM + KV_DIM:Q_DIM + 2 * KV_DIM].reshape(D_MODEL, N_KV_HEADS, 1, HEAD_DIM)
    dup = lambda w: jnp.broadcast_to(w, (D_MODEL, N_KV_HEADS, 2, HEAD_DIM)).reshape(D_MODEL, 2 * KV_DIM)
    w_ext = jnp.concatenate([wq, dup(wk), dup(wv), w_in[:, Q_DIM + 2 * KV_DIM:]], axis=1)
    k = np.arange(FOURIER_GROUP, dtype=np.int64)
    ang = 2.0 * np.pi * ((k[:, None] * k[None, :]) % FOURIER_GROUP) / FOURIER_GROUP
    cs = jnp.asarray(np.concatenate([np.cos(ang), np.sin(ang)], axis=1), F32)
    return dict(
        ffn1=[(p["norm_ffn1"][l], bf(p["ffn1_w_gate"][l]), bf(p["ffn1_w_up"][l]), bf(p["ffn1_w_down"][l]))
              for l in range(2)],
        ffn2=[(p["norm_ffn2"][l], bf(p["ffn2_w_gate"][l]), bf(p["ffn2_w_up"][l]), bf(p["ffn2_w_down"][l]))
              for l in range(2)],
        norm_mix=p["norm_mix"],
        w_ext=bf(w_ext), sink=p["attn_sink"][0], pool_w=bf(p["pool_w"][0]),
        pool_scale=p["pool_scale"][0], w_out_ap=bf(p["w_out_ap"][0]),
        cs=bf(cs), w_out_f=bf(p["w_out_fourier"][0]), norm_final=p["norm_final"],
    )


def _trunk(x, meta, w):
    b, s, _ = x.shape
    flat = lambda a: a.reshape(-1, D_MODEL)
    hm = jnp.broadcast_to(meta[None], (b, N_META, D_MODEL))
    front_pad = lambda a: jnp.pad(a, ((0, 0), (BLOCK - N_META, 0), (0, 0)))

    h = _ffn(flat(x), *w["ffn1"][0]).reshape(b, s, D_MODEL)
    hm = _ffn(flat(hm), *w["ffn1"][0]).reshape(b, N_META, D_MODEL)
    hm_t = front_pad(hm)
    q, kv, u = _proj(h, w["norm_mix"][0], w["w_ext"])
    q_t, kv_t, u_t = _proj(hm_t, w["norm_mix"][0], w["w_ext"])
    mix_args = (w["sink"], w["pool_w"], w["pool_scale"], w["w_out_ap"])
    h_new = _mix(h, q, kv, u, q_t, kv_t, u_t, *mix_args, tail_query=False)
    hm = _mix(hm_t, q, kv, u, q_t, kv_t, u_t, *mix_args, tail_query=True)[:, BLOCK - N_META:]
    h = _ffn(flat(h_new), *w["ffn2"][0]).reshape(b, s, D_MODEL)
    hm = _ffn(flat(hm), *w["ffn2"][0]).reshape(b, N_META, D_MODEL)

    h, hn = _ffn(flat(h), *w["ffn1"][1], g_post=w["norm_mix"][1], post="normed")
    _, hmn = _ffn(flat(hm), *w["ffn1"][1], g_post=w["norm_mix"][1], post="normed")
    h = h.reshape(b, s, D_MODEL)
    hmn_t = front_pad(hmn.reshape(b, N_META, D_MODEL))
    eo = _fold(hn.reshape(b, s, D_MODEL), hmn_t, w["cs"])
    wc, ws = _twiddles(s)
    h = _seq_dft(wc, ws, eo, h, w["w_out_f"])
    return _ffn(flat(h), *w["ffn2"][1], g_post=w["norm_final"], post="final").reshape(b, s, D_MODEL)


def kernel(x_prompt, x_sample, meta_tokens, norm_ffn1, ffn1_w_gate, ffn1_w_up, ffn1_w_down, norm_mix, w_in_ap, attn_sink, pool_w, pool_scale, w_out_ap, w_out_fourier, norm_ffn2, ffn2_w_gate, ffn2_w_up, ffn2_w_down, norm_final):
    w = _prep_weights(dict(
        norm_ffn1=norm_ffn1, ffn1_w_gate=ffn1_w_gate, ffn1_w_up=ffn1_w_up, ffn1_w_down=ffn1_w_down,
        norm_mix=norm_mix, w_in_ap=w_in_ap, attn_sink=attn_sink, pool_w=pool_w, pool_scale=pool_scale,
        w_out_ap=w_out_ap, w_out_fourier=w_out_fourier, norm_ffn2=norm_ffn2, ffn2_w_gate=ffn2_w_gate,
        ffn2_w_up=ffn2_w_up, ffn2_w_down=ffn2_w_down, norm_final=norm_final))
    return (_trunk(x_prompt, meta_tokens, w), _trunk(x_sample, meta_tokens, w))
```

```python
import functools

import numpy as np
import jax
import jax.numpy as jnp
from jax import lax
from jax.experimental import pallas as pl
from jax.experimental.pallas import tpu as pltpu

D_MODEL = 1024
N_META = 16
BLOCK = 128
WINDOW = 128
N_HEADS = 8
N_KV_HEADS = 2
GQA_GROUP = N_HEADS // N_KV_HEADS
HEAD_DIM = 64
Q_DIM = N_HEADS * HEAD_DIM
KV_DIM = N_KV_HEADS * HEAD_DIM
POOL_DIM = D_MODEL // 2
POOL_WINDOWS = (2, 4, 8, 16)
POOL_GROUP = POOL_DIM // len(POOL_WINDOWS)
POOL_HALO = 8
N_FOURIER_GROUPS = 4
FOURIER_GROUP = D_MODEL // N_FOURIER_GROUPS
D_FF = 2816
EPS = 1e-6
NEG_INF = -1e30

F32 = jnp.float32
BF16 = jnp.bfloat16

V7X_VMEM_BYTES = 64 * 1024 * 1024
VMEM_LIMIT_BYTES = V7X_VMEM_BYTES - 8 * 1024 * 1024

FFN_ROWS = 1024
FFN_SUB_ROWS = 512
FFN_COLS = 1024
PROJ_ROWS = 512
MIX_ROWS = 512
ATT_ROWS = 128
DFT_ROWS = 512
FOLD_ROWS = 384


def _params(n_axes):
    return pltpu.CompilerParams(
        dimension_semantics=("arbitrary",) * n_axes,
        vmem_limit_bytes=VMEM_LIMIT_BYTES,
    )


def _resident(shape):
    zeros = (0,) * len(shape)
    return pl.BlockSpec(shape, lambda *_: zeros, pipeline_mode=pl.Buffered(1))


def _rms(x, g):
    y = x * lax.rsqrt(jnp.mean(x * x, axis=-1, keepdims=True) + EPS)
    return y * g


def _dot(a, b):
    return jnp.dot(a, b, preferred_element_type=F32)


def _ffn_kernel(x_ref, g_ref, wg_ref, wu_ref, wd_ref, *rest, post, sub_rows):
    for hf in range(x_ref.shape[0] // sub_rows):
        rows = slice(hf * sub_rows, (hf + 1) * sub_rows)
        x = x_ref[rows, :]
        xn = _rms(x, g_ref[...]).astype(BF16)
        acts = []
        for lo in range(0, D_FF, FFN_COLS):
            hi = min(lo + FFN_COLS, D_FF)
            gate = _dot(xn, wg_ref[:, lo:hi])
            up = _dot(xn, wu_ref[:, lo:hi])
            acts.append((gate * jax.nn.sigmoid(gate) * up).astype(BF16))
        y = x + 0.5 * _dot(jnp.concatenate(acts, axis=1), wd_ref[...])
        if post is None:
            (o_ref,) = rest
            o_ref[rows, :] = y
        elif post == "final":
            gp_ref, o_ref = rest
            o_ref[rows, :] = _rms(y, gp_ref[...])
        else:
            gp_ref, o_ref, on_ref = rest
            o_ref[rows, :] = y
            on_ref[rows, :] = _rms(y, gp_ref[...]).astype(BF16)


def _ffn(x2d, g, wg, wu, wd, g_post=None, post=None):
    n = x2d.shape[0]
    rows = min(FFN_ROWS, n)
    row_spec = pl.BlockSpec((rows, D_MODEL), lambda i: (i, 0))
    in_specs = [row_spec, _resident((1, D_MODEL)), _resident((D_MODEL, D_FF)),
                _resident((D_MODEL, D_FF)), _resident((D_FF, D_MODEL))]
    args = [x2d, g.reshape(1, D_MODEL), wg, wu, wd]
    out_shape = jax.ShapeDtypeStruct((n, D_MODEL), F32)
    out_specs = row_spec
    if post is not None:
        in_specs.append(_resident((1, D_MODEL)))
        args.append(g_post.reshape(1, D_MODEL))
    if post == "normed":
        out_shape = (out_shape, jax.ShapeDtypeStruct((n, D_MODEL), BF16))
        out_specs = (row_spec, row_spec)
    return pl.pallas_call(
        functools.partial(_ffn_kernel, post=post, sub_rows=min(FFN_SUB_ROWS, rows)),
        out_shape=out_shape,
        grid=(pl.cdiv(n, rows),),
        in_specs=in_specs,
        out_specs=out_specs,
        compiler_params=_params(1),
        name="ffn" if post is None else "ffn_" + post,
    )(*args)


PROJ_COLS = Q_DIM + 4 * KV_DIM + POOL_DIM


def _proj_kernel(h_ref, g_ref, w_ref, q_ref, kv_ref, u_ref):
    hn = _rms(h_ref[0], g_ref[...]).astype(BF16)
    z = _dot(hn, w_ref[...])
    q_ref[0] = z[:, :Q_DIM].astype(BF16)
    kv_ref[0] = z[:, Q_DIM:Q_DIM + 4 * KV_DIM].astype(BF16)
    u_ref[0] = z[:, Q_DIM + 4 * KV_DIM:]


def _proj(h3d, g, w_ext):
    b, s, _ = h3d.shape
    rows = min(PROJ_ROWS, s)
    spec = lambda width: pl.BlockSpec((1, rows, width), lambda bi, i: (bi, i, 0))
    return pl.pallas_call(
        _proj_kernel,
        out_shape=(jax.ShapeDtypeStruct((b, s, Q_DIM), BF16),
                   jax.ShapeDtypeStruct((b, s, 4 * KV_DIM), BF16),
                   jax.ShapeDtypeStruct((b, s, POOL_DIM), F32)),
        grid=(b, s // rows),
        in_specs=[spec(D_MODEL), _resident((1, D_MODEL)), _resident((D_MODEL, PROJ_COLS))],
        out_specs=(spec(Q_DIM), spec(4 * KV_DIM), spec(POOL_DIM)),
        compiler_params=_params(2),
        name="mix_proj",
    )(h3d, g.reshape(1, D_MODEL), w_ext)


def _window_sums(u_ext, rows):
    n = u_ext.shape[0]

    def sh(x, k):
        return pltpu.roll(x, k % n, axis=0)

    outs = []
    for gi, w in enumerate(POOL_WINDOWS):
        x = u_ext[:, gi * POOL_GROUP:(gi + 1) * POOL_GROUP]
        acc = x + sh(x, 1)
        half = 1
        while 2 * half < w:
            acc = sh(acc, half) + sh(acc, -half)
            half *= 2
        outs.append(acc[POOL_HALO:POOL_HALO + rows])
    return outs


def _mix_kernel(sink_ref, h_ref, q_ref, kvc_ref, kvp_ref, kvt_ref, kvn_ref,
                uc_ref, up_ref, ut_ref, un_ref, pw_ref, ps_ref, wo_ref, o_ref,
                *, rows, tail_query):
    nblk = rows // BLOCK
    i = pl.program_id(1)
    if tail_query:
        prev_from, cur_from, next_ok = BLOCK, BLOCK - N_META, True
        kv_prev = kvc_ref[0, :BLOCK]
        u_prev = jnp.zeros((POOL_HALO, POOL_DIM), F32)
        u_next = un_ref[0, :POOL_HALO]
        first_local = POOL_HALO + cur_from
        end_local = rows + 2 * POOL_HALO
    else:
        first = i == 0
        last = i == pl.num_programs(1) - 1
        prev_from = jnp.where(first, BLOCK - N_META, 0)
        cur_from, next_ok = 0, jnp.logical_not(last)
        kv_prev = jnp.where(first, kvt_ref[0], kvp_ref[0])
        u_prev = jnp.where(first, ut_ref[0, BLOCK - POOL_HALO:], up_ref[0, BLOCK - POOL_HALO:])
        u_next = jnp.where(last, 0.0, un_ref[0, :POOL_HALO])
        first_local = 0
        end_local = jnp.where(last, rows + POOL_HALO, rows + 2 * POOL_HALO)

    kv_all = jnp.concatenate([kv_prev, kvc_ref[0], kvn_ref[0]], axis=0)
    lane = lax.broadcasted_iota(jnp.int32, (1, BLOCK), 1)
    lo_half = lane < HEAD_DIM
    r_all = lax.broadcasted_iota(jnp.int32, (1, rows + 2 * BLOCK), 1)
    key_lo = jnp.where(prev_from < BLOCK, prev_from, BLOCK + cur_from)
    key_hi = jnp.where(next_ok, rows + 2 * BLOCK, rows + BLOCK)
    key_mask = jnp.where((r_all >= key_lo) & (r_all < key_hi), 0.0, NEG_INF).astype(F32)

    qi = lax.broadcasted_iota(jnp.int32, (BLOCK, 3 * BLOCK), 0)
    ki = lax.broadcasted_iota(jnp.int32, (BLOCK, 3 * BLOCK), 1)
    dist_i = jnp.abs(qi + BLOCK - ki)
    dist = dist_i.astype(F32)
    bias = [jnp.where(dist_i <= WINDOW, -(2.0 ** (-8.0 * (hd + 1) / N_HEADS)) * dist, NEG_INF)
            for hd in range(N_HEADS)]

    q_all = q_ref[0]
    zero = jnp.zeros((), BF16)
    attn_blocks = []
    for n in range(nblk):
        edge = tail_query or n == 0 or n == nblk - 1
        cm = key_mask[:, n * BLOCK:(n + 3) * BLOCK]
        pair_outs = [[] for _ in range(BLOCK // ATT_ROWS)]
        for kh in range(N_KV_HEADS):
            kd = kv_all[n * BLOCK:(n + 3) * BLOCK, kh * 2 * HEAD_DIM:(kh + 1) * 2 * HEAD_DIM]
            v_off = 2 * KV_DIM + kh * 2 * HEAD_DIM
            vd = kv_all[n * BLOCK:(n + 3) * BLOCK, v_off:v_off + 2 * HEAD_DIM]
            v_bd = jnp.concatenate([jnp.where(lo_half, vd, zero),
                                    jnp.where(lo_half, zero, vd)], axis=0)
            for pr in range(GQA_GROUP // 2):
                head0 = kh * GQA_GROUP + 2 * pr
                qp = q_all[n * BLOCK:(n + 1) * BLOCK, head0 * HEAD_DIM:(head0 + 2) * HEAD_DIM]
                qms = (jnp.where(lo_half, qp, zero), jnp.where(lo_half, zero, qp))
                for ri in range(BLOCK // ATT_ROWS):
                    qrows = slice(ri * ATT_ROWS, (ri + 1) * ATT_ROWS)
                    ps, inv = [], []
                    for side in range(2):
                        hd = head0 + side
                        s = lax.dot_general(qms[side][qrows], kd, (((1,), (1,)), ((), ())),
                                            preferred_element_type=F32)
                        s = s + bias[hd][qrows]
                        if edge:
                            s = s + cm
                        sk = sink_ref[hd]
                        m = jnp.maximum(jnp.max(s, axis=-1, keepdims=True), sk)
                        p = jnp.exp(s - m)
                        denom = jnp.sum(p, axis=-1, keepdims=True) + jnp.exp(sk - m)
                        ps.append(p.astype(BF16))
                        inv.append(1.0 / denom)
                    o = _dot(jnp.concatenate(ps, axis=1), v_bd)
                    pair_outs[ri].append(o * jnp.where(lo_half, inv[0], inv[1]))
        attn_blocks += [jnp.concatenate(po, axis=1) for po in pair_outs]
    attn = jnp.concatenate(attn_blocks, axis=0).astype(BF16)

    u_cur = uc_ref[0]
    u_ext = jnp.concatenate([u_prev, u_cur, u_next], axis=0)
    sums = _window_sums(u_ext, rows)
    j = lax.broadcasted_iota(jnp.int32, (rows, 1), 0) + POOL_HALO
    pooled = []
    for gi, w in enumerate(POOL_WINDOWS):
        cnt = jnp.minimum(j + (w - w // 2), end_local) - jnp.maximum(j - w // 2, first_local)
        cnt = jnp.maximum(cnt, 1).astype(F32)
        x = u_cur[:, gi * POOL_GROUP:(gi + 1) * POOL_GROUP]
        pg = (sums[gi] / cnt - x).astype(BF16)
        pooled.append(_dot(pg, pw_ref[gi]))
    y_pool = (jnp.concatenate(pooled, axis=1) * ps_ref[...]).astype(BF16)

    out = h_ref[0] + _dot(attn, wo_ref[:Q_DIM]) + _dot(y_pool, wo_ref[Q_DIM:])
    o_ref[0] = out


def _mix(h, q, kv, u, q_t, kv_t, u_t, sink, pool_w, pool_scale, w_out, *, tail_query):
    if tail_query:
        hq, qq, kvq, uq = h, q_t, kv_t, u_t
    else:
        hq, qq, kvq, uq = h, q, kv, u
    b, s, _ = hq.shape
    rows = min(MIX_ROWS, s)
    rb = rows // BLOCK
    nb_main = kv.shape[1] // BLOCK
    cur = lambda width: pl.BlockSpec((1, rows, width), lambda bi, i: (bi, i, 0))
    blk = lambda width, fn: pl.BlockSpec((1, BLOCK, width), fn)
    if tail_query:
        prev_map = lambda bi, i: (bi, 0, 0)
        next_map = lambda bi, i: (bi, 0, 0)
    else:
        prev_map = lambda bi, i: (bi, jnp.maximum(i * rb - 1, 0), 0)
        next_map = lambda bi, i: (bi, jnp.minimum((i + 1) * rb, nb_main - 1), 0)
    tail_map = lambda bi, i: (bi, 0, 0)
    kvw = 4 * KV_DIM
    in_specs = [
        pl.BlockSpec(memory_space=pltpu.SMEM),
        cur(D_MODEL), cur(Q_DIM),
        cur(kvw), blk(kvw, prev_map), blk(kvw, tail_map), blk(kvw, next_map),
        cur(POOL_DIM), blk(POOL_DIM, prev_map), blk(POOL_DIM, tail_map), blk(POOL_DIM, next_map),
        _resident((len(POOL_WINDOWS), POOL_GROUP, POOL_GROUP)),
        _resident((1, POOL_DIM)),
        _resident((Q_DIM + POOL_DIM, D_MODEL)),
    ]
    return pl.pallas_call(
        functools.partial(_mix_kernel, rows=rows, tail_query=tail_query),
        out_shape=jax.ShapeDtypeStruct((b, s, D_MODEL), F32),
        grid=(b, s // rows),
        in_specs=in_specs,
        out_specs=cur(D_MODEL),
        compiler_params=_params(2),
        name="mix_tail" if tail_query else "mix_main",
    )(sink, hq, qq, kvq, kv, kv_t, kv, uq, u, u_t, u, pool_w, pool_scale.reshape(1, POOL_DIM), w_out)


FOLD_SUB = FOLD_ROWS // BLOCK


def _fold_len(s):
    n_first = (s + N_META) // 2 + 1 + (BLOCK - N_META)
    return pl.cdiv(n_first, FOLD_ROWS) * FOLD_ROWS


def _fold_kernel(*refs, s):
    first = refs[:FOLD_SUB]
    tail_ref = refs[FOLD_SUB]
    win = refs[FOLD_SUB + 1:2 * FOLD_SUB + 2]
    cs_ref, o_ref = refs[2 * FOLD_SUB + 2:]
    j = pl.program_id(1)
    is_first = j == 0

    ri = lax.broadcasted_iota(jnp.int32, (BLOCK, 2 * BLOCK), 0)
    ki = lax.broadcasted_iota(jnp.int32, (BLOCK, 2 * BLOCK), 1)
    rev = jnp.where(ki == 2 * BLOCK - N_META - ri, 1.0, 0.0).astype(BF16)

    zero = jnp.zeros((), BF16)
    w = [r[0] for r in win]
    w[FOLD_SUB] = jnp.where(is_first, zero, w[FOLD_SUB])
    half = s // 2 + BLOCK - N_META // 2
    cs = cs_ref[...]
    for q in range(FOLD_SUB):
        x1 = first[q][0]
        if q == 0:
            x1 = jnp.where(is_first, tail_ref[0], x1)
        pair = jnp.concatenate([w[FOLD_SUB - 1 - q], w[FOLD_SUB - q]], axis=0)
        partner = _dot(rev, pair)
        c = (j * FOLD_SUB + q) * BLOCK + lax.broadcasted_iota(jnp.int32, (BLOCK, 1), 0)
        x1 = jnp.where(c <= half, x1.astype(F32), 0.0)
        partner = jnp.where(c < half, partner, 0.0)
        hs = (x1 + partner).astype(BF16)
        hd = (x1 - partner).astype(BF16)
        rows = slice(q * BLOCK, (q + 1) * BLOCK)
        for gi in range(N_FOURIER_GROUPS):
            lo = gi * FOURIER_GROUP
            e = _dot(hs[:, lo:lo + FOURIER_GROUP], cs[:, :FOURIER_GROUP])
            o = _dot(hd[:, lo:lo + FOURIER_GROUP], cs[:, FOURIER_GROUP:])
            o_ref[0, rows, lo:lo + FOURIER_GROUP] = e.astype(BF16)
            o_ref[0, rows, D_MODEL + lo:D_MODEL + lo + FOURIER_GROUP] = o.astype(BF16)


def _fold(xn, xn_tail, cs):
    b, s, _ = xn.shape
    nb = s // BLOCK
    lf = _fold_len(s)
    blk = lambda fn: pl.BlockSpec((1, BLOCK, D_MODEL), fn)
    first_specs = [blk(lambda bi, j, q=q: (bi, jnp.maximum(j * FOLD_SUB + q - 1, 0), 0))
                   for q in range(FOLD_SUB)]
    win_specs = [blk(lambda bi, j, m=m: (bi, jnp.minimum(nb - FOLD_SUB * (j + 1) + m, nb - 1), 0))
                 for m in range(FOLD_SUB + 1)]
    return pl.pallas_call(
        functools.partial(_fold_kernel, s=s),
        out_shape=jax.ShapeDtypeStruct((b, lf, 2 * D_MODEL), BF16),
        grid=(b, lf // FOLD_ROWS),
        in_specs=first_specs + [blk(lambda bi, j: (bi, 0, 0))] + win_specs
                 + [_resident((FOURIER_GROUP, 2 * FOURIER_GROUP))],
        out_specs=pl.BlockSpec((1, FOLD_ROWS, 2 * D_MODEL), lambda bi, j: (bi, j, 0)),
        compiler_params=_params(2),
        name="fold_dft",
    )(*([xn] * FOLD_SUB), xn_tail, *([xn] * (FOLD_SUB + 1)), cs)


def _twiddle_kernel(ca_ref, sa_ref, cb_ref, sb_ref, wc_ref, ws_ref):
    ca, sa = ca_ref[0], sa_ref[0]
    cb, sb = cb_ref[...], sb_ref[...]
    wc_ref[...] = (ca * cb - sa * sb).astype(BF16)
    ws_ref[...] = (-(sa * cb + ca * sb)).astype(BF16)


def _twiddles(s):
    l_seq = s + N_META
    lf = _fold_len(s)
    nblk = s // BLOCK
    pos = np.arange(lf, dtype=np.int64) - (BLOCK - N_META)
    col_ok = ((pos >= 0) & (pos <= l_seq // 2)).astype(np.float64)
    pos = np.where(col_ok > 0, pos, 0)
    row0 = N_META + BLOCK * np.arange(nblk, dtype=np.int64)
    theta = 2.0 * np.pi / l_seq
    alpha = theta * ((row0[:, None] * pos[None, :]) % l_seq)
    beta = theta * ((np.arange(BLOCK, dtype=np.int64)[:, None] * pos[None, :]) % l_seq)
    scale = col_ok / np.sqrt(float(l_seq) * FOURIER_GROUP)
    ca = jnp.asarray((np.cos(alpha) * scale).reshape(nblk, 1, lf), F32)
    sa = jnp.asarray((np.sin(alpha) * scale).reshape(nblk, 1, lf), F32)
    cb = jnp.asarray(np.cos(beta), F32)
    sb = jnp.asarray(np.sin(beta), F32)
    row_spec = pl.BlockSpec((1, 1, lf), lambda i: (i, 0, 0))
    out_spec = pl.BlockSpec((BLOCK, lf), lambda i: (i, 0))
    return pl.pallas_call(
        _twiddle_kernel,
        out_shape=(jax.ShapeDtypeStruct((s, lf), BF16),) * 2,
        grid=(nblk,),
        in_specs=[row_spec, row_spec, _resident((BLOCK, lf)), _resident((BLOCK, lf))],
        out_specs=(out_spec, out_spec),
        compiler_params=_params(1),
        name="twiddles",
    )(ca, sa, cb, sb)


def _seq_dft_kernel(wc_ref, ws_ref, eo_ref, h_ref, wo_ref, o_ref):
    y = _dot(wc_ref[...], eo_ref[0, :, :D_MODEL]) + _dot(ws_ref[...], eo_ref[0, :, D_MODEL:])
    o_ref[0] = h_ref[0] + _dot(y.astype(BF16), wo_ref[...])


def _seq_dft(wc, ws, eo, h, w_out):
    b, s, _ = h.shape
    lf = eo.shape[1]
    rows = DFT_ROWS
    w_spec = pl.BlockSpec((rows, lf), lambda bi, i: (i, 0))
    h_spec = pl.BlockSpec((1, rows, D_MODEL), lambda bi, i: (bi, i, 0))
    return pl.pallas_call(
        _seq_dft_kernel,
        out_shape=jax.ShapeDtypeStruct((b, s, D_MODEL), F32),
        grid=(b, s // rows),
        in_specs=[w_spec, w_spec,
                  pl.BlockSpec((1, lf, 2 * D_MODEL), lambda bi, i: (bi, 0, 0)),
                  h_spec, _resident((D_MODEL, D_MODEL))],
        out_specs=h_spec,
        compiler_params=_params(2),
        name="seq_dft",
    )(wc, ws, eo, h, w_out)


def _prep_weights(p):
    bf = lambda a: a.astype(BF16)
    w_in = p["w_in_ap"][0]
    wq = w_in[:, :Q_DIM] * (HEAD_DIM ** -0.5)
    wk = w_in[:, Q_DIM:Q_DIM + KV_DIM].reshape(D_MODEL, N_KV_HEADS, 1, HEAD_DIM)
    wv = w_in[:, Q_DIM + KV_DIM:Q_DIM + 2 * KV_DIM].reshape(D_MODEL, N_KV_HEADS, 1, HEAD_DIM)
    dup = lambda w: jnp.broadcast_to(w, (D_MODEL, N_KV_HEADS, 2, HEAD_DIM)).reshape(D_MODEL, 2 * KV_DIM)
    w_ext = jnp.concatenate([wq, dup(wk), dup(wv), w_in[:, Q_DIM + 2 * KV_DIM:]], axis=1)
    k = np.arange(FOURIER_GROUP, dtype=np.int64)
    ang = 2.0 * np.pi * ((k[:, None] * k[None, :]) % FOURIER_GROUP) / FOURIER_GROUP
    cs = jnp.asarray(np.concatenate([np.cos(ang), np.sin(ang)], axis=1), F32)
    return dict(
        ffn1=[(p["norm_ffn1"][l], bf(p["ffn1_w_gate"][l]), bf(p["ffn1_w_up"][l]), bf(p["ffn1_w_down"][l]))
              for l in range(2)],
        ffn2=[(p["norm_ffn2"][l], bf(p["ffn2_w_gate"][l]), bf(p["ffn2_w_up"][l]), bf(p["ffn2_w_down"][l]))
              for l in range(2)],
        norm_mix=p["norm_mix"],
        w_ext=bf(w_ext), sink=p["attn_sink"][0], pool_w=bf(p["pool_w"][0]),
        pool_scale=p["pool_scale"][0], w_out_ap=bf(p["w_out_ap"][0]),
        cs=bf(cs), w_out_f=bf(p["w_out_fourier"][0]), norm_final=p["norm_final"],
    )


def _trunk(x, meta, w):
    b, s, _ = x.shape
    flat = lambda a: a.reshape(-1, D_MODEL)
    hm = jnp.broadcast_to(meta[None], (b, N_META, D_MODEL))
    front_pad = lambda a: jnp.pad(a, ((0, 0), (BLOCK - N_META, 0), (0, 0)))

    h = _ffn(flat(x), *w["ffn1"][0]).reshape(b, s, D_MODEL)
    hm = _ffn(flat(hm), *w["ffn1"][0]).reshape(b, N_META, D_MODEL)
    hm_t = front_pad(hm)
    q, kv, u = _proj(h, w["norm_mix"][0], w["w_ext"])
    q_t, kv_t, u_t = _proj(hm_t, w["norm_mix"][0], w["w_ext"])
    mix_args = (w["sink"], w["pool_w"], w["pool_scale"], w["w_out_ap"])
    h_new = _mix(h, q, kv, u, q_t, kv_t, u_t, *mix_args, tail_query=False)
    hm = _mix(hm_t, q, kv, u, q_t, kv_t, u_t, *mix_args, tail_query=True)[:, BLOCK - N_META:]
    h = _ffn(flat(h_new), *w["ffn2"][0]).reshape(b, s, D_MODEL)
    hm = _ffn(flat(hm), *w["ffn2"][0]).reshape(b, N_META, D_MODEL)

    h, hn = _ffn(flat(h), *w["ffn1"][1], g_post=w["norm_mix"][1], post="normed")
    _, hmn = _ffn(flat(hm), *w["ffn1"][1], g_post=w["norm_mix"][1], post="normed")
    h = h.reshape(b, s, D_MODEL)
    hmn_t = front_pad(hmn.reshape(b, N_META, D_MODEL))
    eo = _fold(hn.reshape(b, s, D_MODEL), hmn_t, w["cs"])
    wc, ws = _twiddles(s)
    h = _seq_dft(wc, ws, eo, h, w["w_out_f"])
    return _ffn(flat(h), *w["ffn2"][1], g_post=w["norm_final"], post="final").reshape(b, s, D_MODEL)


def kernel(x_prompt, x_sample, meta_tokens, norm_ffn1, ffn1_w_gate, ffn1_w_up, ffn1_w_down, norm_mix, w_in_ap, attn_sink, pool_w, pool_scale, w_out_ap, w_out_fourier, norm_ffn2, ffn2_w_gate, ffn2_w_up, ffn2_w_down, norm_final):
    w = _prep_weights(dict(
        norm_ffn1=norm_ffn1, ffn1_w_gate=ffn1_w_gate, ffn1_w_up=ffn1_w_up, ffn1_w_down=ffn1_w_down,
        norm_mix=norm_mix, w_in_ap=w_in_ap, attn_sink=attn_sink, pool_w=pool_w, pool_scale=pool_scale,
        w_out_ap=w_out_ap, w_out_fourier=w_out_fourier, norm_ffn2=norm_ffn2, ffn2_w_gate=ffn2_w_gate,
        ffn2_w_up=ffn2_w_up, ffn2_w_down=ffn2_w_down, norm_final=norm_final))
    return (_trunk(x_prompt, meta_tokens, w), _trunk(x_sample, meta_tokens, w))
```

```python
import functools

import numpy as np
import jax
import jax.numpy as jnp
from jax import lax
from jax.experimental import pallas as pl
from jax.experimental.pallas import tpu as pltpu

D_MODEL = 1024
N_META = 16
BLOCK = 128
WINDOW = 128
N_HEADS = 8
N_KV_HEADS = 2
GQA_GROUP = N_HEADS // N_KV_HEADS
HEAD_DIM = 64
Q_DIM = N_HEADS * HEAD_DIM
KV_DIM = N_KV_HEADS * HEAD_DIM
POOL_DIM = D_MODEL // 2
POOL_WINDOWS = (2, 4, 8, 16)
POOL_GROUP = POOL_DIM // len(POOL_WINDOWS)
POOL_HALO = 8
N_FOURIER_GROUPS = 4
FOURIER_GROUP = D_MODEL // N_FOURIER_GROUPS
D_FF = 2816
EPS = 1e-6
NEG_INF = -1e30

F32 = jnp.float32
BF16 = jnp.bfloat16

V7X_VMEM_BYTES = 64 * 1024 * 1024
VMEM_LIMIT_BYTES = V7X_VMEM_BYTES - 8 * 1024 * 1024

FFN_ROWS = 1024
FFN_SUB_ROWS = 512
FFN_COLS = 256
PROJ_ROWS = 512
MIX_ROWS = 512
ATT_ROWS = 128
DFT_ROWS = 512
FOLD_ROWS = 384


def _params(n_axes):
    return pltpu.CompilerParams(
        dimension_semantics=("arbitrary",) * n_axes,
        vmem_limit_bytes=VMEM_LIMIT_BYTES,
    )


def _resident(shape):
    zeros = (0,) * len(shape)
    return pl.BlockSpec(shape, lambda *_: zeros, pipeline_mode=pl.Buffered(1))


def _rms(x, g):
    y = x * lax.rsqrt(jnp.mean(x * x, axis=-1, keepdims=True) + EPS)
    return y * g


def _dot(a, b):
    return jnp.dot(a, b, preferred_element_type=F32)


def _ffn_kernel(x_ref, g_ref, wg_ref, wu_ref, wd_ref, *rest, post, sub_rows):
    for hf in range(x_ref.shape[0] // sub_rows):
        rows = slice(hf * sub_rows, (hf + 1) * sub_rows)
        x = x_ref[rows, :]
        xn = _rms(x, g_ref[...]).astype(BF16)
        acts = []
        for lo in range(0, D_FF, FFN_COLS):
            hi = min(lo + FFN_COLS, D_FF)
            gate = _dot(xn, wg_ref[:, lo:hi])
            up = _dot(xn, wu_ref[:, lo:hi])
            acts.append((gate * jax.nn.sigmoid(gate) * up).astype(BF16))
        y = x + 0.5 * _dot(jnp.concatenate(acts, axis=1), wd_ref[...])
        if post is None:
            (o_ref,) = rest
            o_ref[rows, :] = y
        elif post == "final":
            gp_ref, o_ref = rest
            o_ref[rows, :] = _rms(y, gp_ref[...])
        else:
            gp_ref, o_ref, on_ref = rest
            o_ref[rows, :] = y
            on_ref[rows, :] = _rms(y, gp_ref[...]).astype(BF16)


def _ffn(x2d, g, wg, wu, wd, g_post=None, post=None):
    n = x2d.shape[0]
    rows = min(FFN_ROWS, n)
    row_spec = pl.BlockSpec((rows, D_MODEL), lambda i: (i, 0))
    in_specs = [row_spec, _resident((1, D_MODEL)), _resident((D_MODEL, D_FF)),
                _resident((D_MODEL, D_FF)), _resident((D_FF, D_MODEL))]
    args = [x2d, g.reshape(1, D_MODEL), wg, wu, wd]
    out_shape = jax.ShapeDtypeStruct((n, D_MODEL), F32)
    out_specs = row_spec
    if post is not None:
        in_specs.append(_resident((1, D_MODEL)))
        args.append(g_post.reshape(1, D_MODEL))
    if post == "normed":
        out_shape = (out_shape, jax.ShapeDtypeStruct((n, D_MODEL), BF16))
        out_specs = (row_spec, row_spec)
    return pl.pallas_call(
        functools.partial(_ffn_kernel, post=post, sub_rows=min(FFN_SUB_ROWS, rows)),
        out_shape=out_shape,
        grid=(pl.cdiv(n, rows),),
        in_specs=in_specs,
        out_specs=out_specs,
        compiler_params=_params(1),
        name="ffn" if post is None else "ffn_" + post,
    )(*args)


PROJ_COLS = Q_DIM + 4 * KV_DIM + POOL_DIM


def _proj_kernel(h_ref, g_ref, w_ref, q_ref, kv_ref, u_ref):
    hn = _rms(h_ref[0], g_ref[...]).astype(BF16)
    z = _dot(hn, w_ref[...])
    q_ref[0] = z[:, :Q_DIM].astype(BF16)
    kv_ref[0] = z[:, Q_DIM:Q_DIM + 4 * KV_DIM].astype(BF16)
    u_ref[0] = z[:, Q_DIM + 4 * KV_DIM:]


def _proj(h3d, g, w_ext):
    b, s, _ = h3d.shape
    rows = min(PROJ_ROWS, s)
    spec = lambda width: pl.BlockSpec((1, rows, width), lambda bi, i: (bi, i, 0))
    return pl.pallas_call(
        _proj_kernel,
        out_shape=(jax.ShapeDtypeStruct((b, s, Q_DIM), BF16),
                   jax.ShapeDtypeStruct((b, s, 4 * KV_DIM), BF16),
                   jax.ShapeDtypeStruct((b, s, POOL_DIM), F32)),
        grid=(b, s // rows),
        in_specs=[spec(D_MODEL), _resident((1, D_MODEL)), _resident((D_MODEL, PROJ_COLS))],
        out_specs=(spec(Q_DIM), spec(4 * KV_DIM), spec(POOL_DIM)),
        compiler_params=_params(2),
        name="mix_proj",
    )(h3d, g.reshape(1, D_MODEL), w_ext)


def _window_sums(u_ext, rows):
    n = u_ext.shape[0]

    def sh(x, k):
        return pltpu.roll(x, k % n, axis=0)

    outs = []
    for gi, w in enumerate(POOL_WINDOWS):
        x = u_ext[:, gi * POOL_GROUP:(gi + 1) * POOL_GROUP]
        acc = x + sh(x, 1)
        half = 1
        while 2 * half < w:
            acc = sh(acc, half) + sh(acc, -half)
            half *= 2
        outs.append(acc[POOL_HALO:POOL_HALO + rows])
    return outs


def _mix_kernel(sink_ref, h_ref, q_ref, kvc_ref, kvp_ref, kvt_ref, kvn_ref,
                uc_ref, up_ref, ut_ref, un_ref, pw_ref, ps_ref, wo_ref, o_ref,
                *, rows, tail_query):
    nblk = rows // BLOCK
    i = pl.program_id(1)
    if tail_query:
        prev_from, cur_from, next_ok = BLOCK, BLOCK - N_META, True
        kv_prev = kvc_ref[0, :BLOCK]
        u_prev = jnp.zeros((POOL_HALO, POOL_DIM), F32)
        u_next = un_ref[0, :POOL_HALO]
        first_local = POOL_HALO + cur_from
        end_local = rows + 2 * POOL_HALO
    else:
        first = i == 0
        last = i == pl.num_programs(1) - 1
        prev_from = jnp.where(first, BLOCK - N_META, 0)
        cur_from, next_ok = 0, jnp.logical_not(last)
        kv_prev = jnp.where(first, kvt_ref[0], kvp_ref[0])
        u_prev = jnp.where(first, ut_ref[0, BLOCK - POOL_HALO:], up_ref[0, BLOCK - POOL_HALO:])
        u_next = jnp.where(last, 0.0, un_ref[0, :POOL_HALO])
        first_local = 0
        end_local = jnp.where(last, rows + POOL_HALO, rows + 2 * POOL_HALO)

    kv_all = jnp.concatenate([kv_prev, kvc_ref[0], kvn_ref[0]], axis=0)
    lane = lax.broadcasted_iota(jnp.int32, (1, BLOCK), 1)
    lo_half = lane < HEAD_DIM
    r_all = lax.broadcasted_iota(jnp.int32, (1, rows + 2 * BLOCK), 1)
    key_lo = jnp.where(prev_from < BLOCK, prev_from, BLOCK + cur_from)
    key_hi = jnp.where(next_ok, rows + 2 * BLOCK, rows + BLOCK)
    key_mask = jnp.where((r_all >= key_lo) & (r_all < key_hi), 0.0, NEG_INF).astype(F32)

    qi = lax.broadcasted_iota(jnp.int32, (BLOCK, 3 * BLOCK), 0)
    ki = lax.broadcasted_iota(jnp.int32, (BLOCK, 3 * BLOCK), 1)
    dist_i = jnp.abs(qi + BLOCK - ki)
    dist = dist_i.astype(F32)
    bias = [jnp.where(dist_i <= WINDOW, -(2.0 ** (-8.0 * (hd + 1) / N_HEADS)) * dist, NEG_INF)
            for hd in range(N_HEADS)]

    q_all = q_ref[0]
    zero = jnp.zeros((), BF16)
    attn_blocks = []
    for n in range(nblk):
        edge = tail_query or n == 0 or n == nblk - 1
        cm = key_mask[:, n * BLOCK:(n + 3) * BLOCK]
        pair_outs = [[] for _ in range(BLOCK // ATT_ROWS)]
        for kh in range(N_KV_HEADS):
            kd = kv_all[n * BLOCK:(n + 3) * BLOCK, kh * 2 * HEAD_DIM:(kh + 1) * 2 * HEAD_DIM]
            v_off = 2 * KV_DIM + kh * 2 * HEAD_DIM
            vd = kv_all[n * BLOCK:(n + 3) * BLOCK, v_off:v_off + 2 * HEAD_DIM]
            v_bd = jnp.concatenate([jnp.where(lo_half, vd, zero),
                                    jnp.where(lo_half, zero, vd)], axis=0)
            for pr in range(GQA_GROUP // 2):
                head0 = kh * GQA_GROUP + 2 * pr
                qp = q_all[n * BLOCK:(n + 1) * BLOCK, head0 * HEAD_DIM:(head0 + 2) * HEAD_DIM]
                qms = (jnp.where(lo_half, qp, zero), jnp.where(lo_half, zero, qp))
                for ri in range(BLOCK // ATT_ROWS):
                    qrows = slice(ri * ATT_ROWS, (ri + 1) * ATT_ROWS)
                    ps, inv = [], []
                    for side in range(2):
                        hd = head0 + side
                        s = lax.dot_general(qms[side][qrows], kd, (((1,), (1,)), ((), ())),
                                            preferred_element_type=F32)
                        s = s + bias[hd][qrows]
                        if edge:
                            s = s + cm
                        sk = sink_ref[hd]
                        m = jnp.maximum(jnp.max(s, axis=-1, keepdims=True), sk)
                        p = jnp.exp(s - m)
                        denom = jnp.sum(p, axis=-1, keepdims=True) + jnp.exp(sk - m)
                        ps.append(p.astype(BF16))
                        inv.append(1.0 / denom)
                    o = _dot(jnp.concatenate(ps, axis=1), v_bd)
                    pair_outs[ri].append(o * jnp.where(lo_half, inv[0], inv[1]))
        attn_blocks += [jnp.concatenate(po, axis=1) for po in pair_outs]
    attn = jnp.concatenate(attn_blocks, axis=0).astype(BF16)

    u_cur = uc_ref[0]
    u_ext = jnp.concatenate([u_prev, u_cur, u_next], axis=0)
    sums = _window_sums(u_ext, rows)
    j = lax.broadcasted_iota(jnp.int32, (rows, 1), 0) + POOL_HALO
    pooled = []
    for gi, w in enumerate(POOL_WINDOWS):
        cnt = jnp.minimum(j + (w - w // 2), end_local) - jnp.maximum(j - w // 2, first_local)
        cnt = jnp.maximum(cnt, 1).astype(F32)
        x = u_cur[:, gi * POOL_GROUP:(gi + 1) * POOL_GROUP]
        pg = (sums[gi] / cnt - x).astype(BF16)
        pooled.append(_dot(pg, pw_ref[gi]))
    y_pool = (jnp.concatenate(pooled, axis=1) * ps_ref[...]).astype(BF16)

    out = h_ref[0] + _dot(attn, wo_ref[:Q_DIM]) + _dot(y_pool, wo_ref[Q_DIM:])
    o_ref[0] = out


def _mix(h, q, kv, u, q_t, kv_t, u_t, sink, pool_w, pool_scale, w_out, *, tail_query):
    if tail_query:
        hq, qq, kvq, uq = h, q_t, kv_t, u_t
    else:
        hq, qq, kvq, uq = h, q, kv, u
    b, s, _ = hq.shape
    rows = min(MIX_ROWS, s)
    rb = rows // BLOCK
    nb_main = kv.shape[1] // BLOCK
    cur = lambda width: pl.BlockSpec((1, rows, width), lambda bi, i: (bi, i, 0))
    blk = lambda width, fn: pl.BlockSpec((1, BLOCK, width), fn)
    if tail_query:
        prev_map = lambda bi, i: (bi, 0, 0)
        next_map = lambda bi, i: (bi, 0, 0)
    else:
        prev_map = lambda bi, i: (bi, jnp.maximum(i * rb - 1, 0), 0)
        next_map = lambda bi, i: (bi, jnp.minimum((i + 1) * rb, nb_main - 1), 0)
    tail_map = lambda bi, i: (bi, 0, 0)
    kvw = 4 * KV_DIM
    in_specs = [
        pl.BlockSpec(memory_space=pltpu.SMEM),
        cur(D_MODEL), cur(Q_DIM),
        cur(kvw), blk(kvw, prev_map), blk(kvw, tail_map), blk(kvw, next_map),
        cur(POOL_DIM), blk(POOL_DIM, prev_map), blk(POOL_DIM, tail_map), blk(POOL_DIM, next_map),
        _resident((len(POOL_WINDOWS), POOL_GROUP, POOL_GROUP)),
        _resident((1, POOL_DIM)),
        _resident((Q_DIM + POOL_DIM, D_MODEL)),
    ]
    return pl.pallas_call(
        functools.partial(_mix_kernel, rows=rows, tail_query=tail_query),
        out_shape=jax.ShapeDtypeStruct((b, s, D_MODEL), F32),
        grid=(b, s // rows),
        in_specs=in_specs,
        out_specs=cur(D_MODEL),
        compiler_params=_params(2),
        name="mix_tail" if tail_query else "mix_main",
    )(sink, hq, qq, kvq, kv, kv_t, kv, uq, u, u_t, u, pool_w, pool_scale.reshape(1, POOL_DIM), w_out)


FOLD_SUB = FOLD_ROWS // BLOCK


def _fold_len(s):
    n_first = (s + N_META) // 2 + 1 + (BLOCK - N_META)
    return pl.cdiv(n_first, FOLD_ROWS) * FOLD_ROWS


def _fold_kernel(*refs, s):
    first = refs[:FOLD_SUB]
    tail_ref = refs[FOLD_SUB]
    win = refs[FOLD_SUB + 1:2 * FOLD_SUB + 2]
    cs_ref, o_ref = refs[2 * FOLD_SUB + 2:]
    j = pl.program_id(1)
    is_first = j == 0

    ri = lax.broadcasted_iota(jnp.int32, (BLOCK, 2 * BLOCK), 0)
    ki = lax.broadcasted_iota(jnp.int32, (BLOCK, 2 * BLOCK), 1)
    rev = jnp.where(ki == 2 * BLOCK - N_META - ri, 1.0, 0.0).astype(BF16)

    zero = jnp.zeros((), BF16)
    w = [r[0] for r in win]
    w[FOLD_SUB] = jnp.where(is_first, zero, w[FOLD_SUB])
    half = s // 2 + BLOCK - N_META // 2
    cs = cs_ref[...]
    for q in range(FOLD_SUB):
        x1 = first[q][0]
        if q == 0:
            x1 = jnp.where(is_first, tail_ref[0], x1)
        pair = jnp.concatenate([w[FOLD_SUB - 1 - q], w[FOLD_SUB - q]], axis=0)
        partner = _dot(rev, pair)
        c = (j * FOLD_SUB + q) * BLOCK + lax.broadcasted_iota(jnp.int32, (BLOCK, 1), 0)
        x1 = jnp.where(c <= half, x1.astype(F32), 0.0)
        partner = jnp.where(c < half, partner, 0.0)
        hs = (x1 + partner).astype(BF16)
        hd = (x1 - partner).astype(BF16)
        rows = slice(q * BLOCK, (q + 1) * BLOCK)
        for gi in range(N_FOURIER_GROUPS):
            lo = gi * FOURIER_GROUP
            e = _dot(hs[:, lo:lo + FOURIER_GROUP], cs[:, :FOURIER_GROUP])
            o = _dot(hd[:, lo:lo + FOURIER_GROUP], cs[:, FOURIER_GROUP:])
            o_ref[0, rows, lo:lo + FOURIER_GROUP] = e.astype(BF16)
            o_ref[0, rows, D_MODEL + lo:D_MODEL + lo + FOURIER_GROUP] = o.astype(BF16)


def _fold(xn, xn_tail, cs):
    b, s, _ = xn.shape
    nb = s // BLOCK
    lf = _fold_len(s)
    blk = lambda fn: pl.BlockSpec((1, BLOCK, D_MODEL), fn)
    first_specs = [blk(lambda bi, j, q=q: (bi, jnp.maximum(j * FOLD_SUB + q - 1, 0), 0))
                   for q in range(FOLD_SUB)]
    win_specs = [blk(lambda bi, j, m=m: (bi, jnp.minimum(nb - FOLD_SUB * (j + 1) + m, nb - 1), 0))
                 for m in range(FOLD_SUB + 1)]
    return pl.pallas_call(
        functools.partial(_fold_kernel, s=s),
        out_shape=jax.ShapeDtypeStruct((b, lf, 2 * D_MODEL), BF16),
        grid=(b, lf // FOLD_ROWS),
        in_specs=first_specs + [blk(lambda bi, j: (bi, 0, 0))] + win_specs
                 + [_resident((FOURIER_GROUP, 2 * FOURIER_GROUP))],
        out_specs=pl.BlockSpec((1, FOLD_ROWS, 2 * D_MODEL), lambda bi, j: (bi, j, 0)),
        compiler_params=_params(2),
        name="fold_dft",
    )(*([xn] * FOLD_SUB), xn_tail, *([xn] * (FOLD_SUB + 1)), cs)


def _twiddle_kernel(ca_ref, sa_ref, cb_ref, sb_ref, wc_ref, ws_ref):
    ca, sa = ca_ref[0], sa_ref[0]
    cb, sb = cb_ref[...], sb_ref[...]
    wc_ref[...] = (ca * cb - sa * sb).astype(BF16)
    ws_ref[...] = (-(sa * cb + ca * sb)).astype(BF16)


def _twiddles(s):
    l_seq = s + N_META
    lf = _fold_len(s)
    nblk = s // BLOCK
    pos = np.arange(lf, dtype=np.int64) - (BLOCK - N_META)
    col_ok = ((pos >= 0) & (pos <= l_seq // 2)).astype(np.float64)
    pos = np.where(col_ok > 0, pos, 0)
    row0 = N_META + BLOCK * np.arange(nblk, dtype=np.int64)
    theta = 2.0 * np.pi / l_seq
    alpha = theta * ((row0[:, None] * pos[None, :]) % l_seq)
    beta = theta * ((np.arange(BLOCK, dtype=np.int64)[:, None] * pos[None, :]) % l_seq)
    scale = col_ok / np.sqrt(float(l_seq) * FOURIER_GROUP)
    ca = jnp.asarray((np.cos(alpha) * scale).reshape(nblk, 1, lf), F32)
    sa = jnp.asarray((np.sin(alpha) * scale).reshape(nblk, 1, lf), F32)
    cb = jnp.asarray(np.cos(beta), F32)
    sb = jnp.asarray(np.sin(beta), F32)
    row_spec = pl.BlockSpec((1, 1, lf), lambda i: (i, 0, 0))
    out_spec = pl.BlockSpec((BLOCK, lf), lambda i: (i, 0))
    return pl.pallas_call(
        _twiddle_kernel,
        out_shape=(jax.ShapeDtypeStruct((s, lf), BF16),) * 2,
        grid=(nblk,),
        in_specs=[row_spec, row_spec, _resident((BLOCK, lf)), _resident((BLOCK, lf))],
        out_specs=(out_spec, out_spec),
        compiler_params=_params(1),
        name="twiddles",
    )(ca, sa, cb, sb)


def _seq_dft_kernel(wc_ref, ws_ref, eo_ref, h_ref, wo_ref, o_ref):
    y = _dot(wc_ref[...], eo_ref[0, :, :D_MODEL]) + _dot(ws_ref[...], eo_ref[0, :, D_MODEL:])
    o_ref[0] = h_ref[0] + _dot(y.astype(BF16), wo_ref[...])


def _seq_dft(wc, ws, eo, h, w_out):
    b, s, _ = h.shape
    lf = eo.shape[1]
    rows = DFT_ROWS
    w_spec = pl.BlockSpec((rows, lf), lambda bi, i: (i, 0))
    h_spec = pl.BlockSpec((1, rows, D_MODEL), lambda bi, i: (bi, i, 0))
    return pl.pallas_call(
        _seq_dft_kernel,
        out_shape=jax.ShapeDtypeStruct((b, s, D_MODEL), F32),
        grid=(b, s // rows),
        in_specs=[w_spec, w_spec,
                  pl.BlockSpec((1, lf, 2 * D_MODEL), lambda bi, i: (bi, 0, 0)),
                  h_spec, _resident((D_MODEL, D_MODEL))],
        out_specs=h_spec,
        compiler_params=_params(2),
        name="seq_dft",
    )(wc, ws, eo, h, w_out)


def _prep_weights(p):
    bf = lambda a: a.astype(BF16)
    w_in = p["w_in_ap"][0]
    wq = w_in[:, :Q_DIM] * (HEAD_DIM ** -0.5)
    wk = w_in[:, Q_DIM:Q_DIM + KV_DIM].reshape(D_MODEL, N_KV_HEADS, 1, HEAD_DIM)
    wv = w_in[:, Q_DIM + KV_DIM:Q_DIM + 2 * KV_DIM].reshape(D_MODEL, N_KV_HEADS, 1, HEAD_DIM)
    dup = lambda w: jnp.broadcast_to(w, (D_MODEL, N_KV_HEADS, 2, HEAD_DIM)).reshape(D_MODEL, 2 * KV_DIM)
    w_ext = jnp.concatenate([wq, dup(wk), dup(wv), w_in[:, Q_DIM + 2 * KV_DIM:]], axis=1)
    k = np.arange(FOURIER_GROUP, dtype=np.int64)
    ang = 2.0 * np.pi * ((k[:, None] * k[None, :]) % FOURIER_GROUP) / FOURIER_GROUP
    cs = jnp.asarray(np.concatenate([np.cos(ang), np.sin(ang)], axis=1), F32)
    return dict(
        ffn1=[(p["norm_ffn1"][l], bf(p["ffn1_w_gate"][l]), bf(p["ffn1_w_up"][l]), bf(p["ffn1_w_down"][l]))
              for l in range(2)],
        ffn2=[(p["norm_ffn2"][l], bf(p["ffn2_w_gate"][l]), bf(p["ffn2_w_up"][l]), bf(p["ffn2_w_down"][l]))
              for l in range(2)],
        norm_mix=p["norm_mix"],
        w_ext=bf(w_ext), sink=p["attn_sink"][0], pool_w=bf(p["pool_w"][0]),
        pool_scale=p["pool_scale"][0], w_out_ap=bf(p["w_out_ap"][0]),
        cs=bf(cs), w_out_f=bf(p["w_out_fourier"][0]), norm_final=p["norm_final"],
    )


def _trunk(x, meta, w):
    b, s, _ = x.shape
    flat = lambda a: a.reshape(-1, D_MODEL)
    hm = jnp.broadcast_to(meta[None], (b, N_META, D_MODEL))
    front_pad = lambda a: jnp.pad(a, ((0, 0), (BLOCK - N_META, 0), (0, 0)))

    h = _ffn(flat(x), *w["ffn1"][0]).reshape(b, s, D_MODEL)
    hm = _ffn(flat(hm), *w["ffn1"][0]).reshape(b, N_META, D_MODEL)
    hm_t = front_pad(hm)
    q, kv, u = _proj(h, w["norm_mix"][0], w["w_ext"])
    q_t, kv_t, u_t = _proj(hm_t, w["norm_mix"][0], w["w_ext"])
    mix_args = (w["sink"], w["pool_w"], w["pool_scale"], w["w_out_ap"])
    h_new = _mix(h, q, kv, u, q_t, kv_t, u_t, *mix_args, tail_query=False)
    hm = _mix(hm_t, q, kv, u, q_t, kv_t, u_t, *mix_args, tail_query=True)[:, BLOCK - N_META:]
    h = _ffn(flat(h_new), *w["ffn2"][0]).reshape(b, s, D_MODEL)
    hm = _ffn(flat(hm), *w["ffn2"][0]).reshape(b, N_META, D_MODEL)

    h, hn = _ffn(flat(h), *w["ffn1"][1], g_post=w["norm_mix"][1], post="normed")
    _, hmn = _ffn(flat(hm), *w["ffn1"][1], g_post=w["norm_mix"][1], post="normed")
    h = h.reshape(b, s, D_MODEL)
    hmn_t = front_pad(hmn.reshape(b, N_META, D_MODEL))
    eo = _fold(hn.reshape(b, s, D_MODEL), hmn_t, w["cs"])
    wc, ws = _twiddles(s)
    h = _seq_dft(wc, ws, eo, h, w["w_out_f"])
    return _ffn(flat(h), *w["ffn2"][1], g_post=w["norm_final"], post="final").reshape(b, s, D_MODEL)


def kernel(x_prompt, x_sample, meta_tokens, norm_ffn1, ffn1_w_gate, ffn1_w_up, ffn1_w_down, norm_mix, w_in_ap, attn_sink, pool_w, pool_scale, w_out_ap, w_out_fourier, norm_ffn2, ffn2_w_gate, ffn2_w_up, ffn2_w_down, norm_final):
    w = _prep_weights(dict(
        norm_ffn1=norm_ffn1, ffn1_w_gate=ffn1_w_gate, ffn1_w_up=ffn1_w_up, ffn1_w_down=ffn1_w_down,
        norm_mix=norm_mix, w_in_ap=w_in_ap, attn_sink=attn_sink, pool_w=pool_w, pool_scale=pool_scale,
        w_out_ap=w_out_ap, w_out_fourier=w_out_fourier, norm_ffn2=norm_ffn2, ffn2_w_gate=ffn2_w_gate,
        ffn2_w_up=ffn2_w_up, ffn2_w_down=ffn2_w_down, norm_final=norm_final))
    return (_trunk(x_prompt, meta_tokens, w), _trunk(x_sample, meta_tokens, w))
```

```python
import functools

import numpy as np
import jax
import jax.numpy as jnp
from jax import lax
from jax.experimental import pallas as pl
from jax.experimental.pallas import tpu as pltpu

D_MODEL = 1024
N_META = 16
BLOCK = 128
WINDOW = 128
N_HEADS = 8
N_KV_HEADS = 2
GQA_GROUP = N_HEADS // N_KV_HEADS
HEAD_DIM = 64
Q_DIM = N_HEADS * HEAD_DIM
KV_DIM = N_KV_HEADS * HEAD_DIM
POOL_DIM = D_MODEL // 2
POOL_WINDOWS = (2, 4, 8, 16)
POOL_GROUP = POOL_DIM // len(POOL_WINDOWS)
POOL_HALO = 8
N_FOURIER_GROUPS = 4
FOURIER_GROUP = D_MODEL // N_FOURIER_GROUPS
D_FF = 2816
EPS = 1e-6
NEG_INF = -1e30

F32 = jnp.float32
BF16 = jnp.bfloat16

V7X_VMEM_BYTES = 64 * 1024 * 1024
VMEM_LIMIT_BYTES = V7X_VMEM_BYTES - 8 * 1024 * 1024

FFN_ROWS = 1024
FFN_SUB_ROWS = 512
FFN_COLS = 256
PROJ_ROWS = 512
MIX_ROWS = 512
ATT_ROWS = 128
DFT_ROWS = 512
FOLD_ROWS = 384


def _params(n_axes):
    return pltpu.CompilerParams(
        dimension_semantics=("arbitrary",) * n_axes,
        vmem_limit_bytes=VMEM_LIMIT_BYTES,
    )


def _resident(shape):
    zeros = (0,) * len(shape)
    return pl.BlockSpec(shape, lambda *_: zeros, pipeline_mode=pl.Buffered(1))


def _rms(x, g):
    y = x * lax.rsqrt(jnp.mean(x * x, axis=-1, keepdims=True) + EPS)
    return y * g


def _dot(a, b):
    return jnp.dot(a, b, preferred_element_type=F32)


def _ffn_kernel(x_ref, g_ref, wg_ref, wu_ref, wd_ref, *rest, post, sub_rows):
    for hf in range(x_ref.shape[0] // sub_rows):
        rows = slice(hf * sub_rows, (hf + 1) * sub_rows)
        x = x_ref[rows, :]
        xn = _rms(x, g_ref[...]).astype(BF16)
        acts = []
        for lo in range(0, D_FF, FFN_COLS):
            hi = min(lo + FFN_COLS, D_FF)
            gate = _dot(xn, wg_ref[:, lo:hi])
            up = _dot(xn, wu_ref[:, lo:hi])
            acts.append((gate * jax.nn.sigmoid(gate) * up).astype(BF16))
        y = x + 0.5 * _dot(jnp.concatenate(acts, axis=1), wd_ref[...])
        if post is None:
            (o_ref,) = rest
            o_ref[rows, :] = y
        elif post == "final":
            gp_ref, o_ref = rest
            o_ref[rows, :] = _rms(y, gp_ref[...])
        elif post == "proj":
            gp_ref, wp_ref, o_ref, q_ref, kv_ref, u_ref = rest
            o_ref[rows, :] = y
            z = _dot(_rms(y, gp_ref[...]).astype(BF16), wp_ref[...])
            q_ref[rows, :] = z[:, :Q_DIM].astype(BF16)
            kv_ref[rows, :] = z[:, Q_DIM:Q_DIM + 4 * KV_DIM].astype(BF16)
            u_ref[rows, :] = z[:, Q_DIM + 4 * KV_DIM:]
        else:
            gp_ref, o_ref, on_ref = rest
            o_ref[rows, :] = y
            on_ref[rows, :] = _rms(y, gp_ref[...]).astype(BF16)


def _ffn(x2d, g, wg, wu, wd, g_post=None, post=None, w_proj=None):
    n = x2d.shape[0]
    rows = min(FFN_ROWS, n)
    row_spec = pl.BlockSpec((rows, D_MODEL), lambda i: (i, 0))
    in_specs = [row_spec, _resident((1, D_MODEL)), _resident((D_MODEL, D_FF)),
                _resident((D_MODEL, D_FF)), _resident((D_FF, D_MODEL))]
    args = [x2d, g.reshape(1, D_MODEL), wg, wu, wd]
    out_shape = jax.ShapeDtypeStruct((n, D_MODEL), F32)
    out_specs = row_spec
    if post is not None:
        in_specs.append(_resident((1, D_MODEL)))
        args.append(g_post.reshape(1, D_MODEL))
    if post == "normed":
        out_shape = (out_shape, jax.ShapeDtypeStruct((n, D_MODEL), BF16))
        out_specs = (row_spec, row_spec)
    if post == "proj":
        widths = (Q_DIM, 4 * KV_DIM, POOL_DIM)
        in_specs.append(_resident(w_proj.shape))
        args.append(w_proj)
        out_shape = (out_shape,) + tuple(jax.ShapeDtypeStruct((n, wd_), dt)
                                         for wd_, dt in zip(widths, (BF16, BF16, F32)))
        out_specs = (row_spec,) + tuple(pl.BlockSpec((rows, wd_), lambda i: (i, 0)) for wd_ in widths)
    return pl.pallas_call(
        functools.partial(_ffn_kernel, post=post, sub_rows=min(FFN_SUB_ROWS, rows)),
        out_shape=out_shape,
        grid=(pl.cdiv(n, rows),),
        in_specs=in_specs,
        out_specs=out_specs,
        compiler_params=_params(1),
        name="ffn" if post is None else "ffn_" + post,
    )(*args)


PROJ_COLS = Q_DIM + 4 * KV_DIM + POOL_DIM


def _proj_kernel(h_ref, g_ref, w_ref, q_ref, kv_ref, u_ref):
    hn = _rms(h_ref[0], g_ref[...]).astype(BF16)
    z = _dot(hn, w_ref[...])
    q_ref[0] = z[:, :Q_DIM].astype(BF16)
    kv_ref[0] = z[:, Q_DIM:Q_DIM + 4 * KV_DIM].astype(BF16)
    u_ref[0] = z[:, Q_DIM + 4 * KV_DIM:]


def _proj(h3d, g, w_ext):
    b, s, _ = h3d.shape
    rows = min(PROJ_ROWS, s)
    spec = lambda width: pl.BlockSpec((1, rows, width), lambda bi, i: (bi, i, 0))
    return pl.pallas_call(
        _proj_kernel,
        out_shape=(jax.ShapeDtypeStruct((b, s, Q_DIM), BF16),
                   jax.ShapeDtypeStruct((b, s, 4 * KV_DIM), BF16),
                   jax.ShapeDtypeStruct((b, s, POOL_DIM), F32)),
        grid=(b, s // rows),
        in_specs=[spec(D_MODEL), _resident((1, D_MODEL)), _resident((D_MODEL, PROJ_COLS))],
        out_specs=(spec(Q_DIM), spec(4 * KV_DIM), spec(POOL_DIM)),
        compiler_params=_params(2),
        name="mix_proj",
    )(h3d, g.reshape(1, D_MODEL), w_ext)


def _window_sums(u_ext, rows):
    n = u_ext.shape[0]

    def sh(x, k):
        return pltpu.roll(x, k % n, axis=0)

    outs = []
    for gi, w in enumerate(POOL_WINDOWS):
        x = u_ext[:, gi * POOL_GROUP:(gi + 1) * POOL_GROUP]
        acc = x + sh(x, 1)
        half = 1
        while 2 * half < w:
            acc = sh(acc, half) + sh(acc, -half)
            half *= 2
        outs.append(acc[POOL_HALO:POOL_HALO + rows])
    return outs


def _mix_kernel(sink_ref, h_ref, q_ref, kvc_ref, kvp_ref, kvt_ref, kvn_ref,
                uc_ref, up_ref, ut_ref, un_ref, pw_ref, ps_ref, wo_ref, o_ref,
                *, rows, tail_query):
    nblk = rows // BLOCK
    i = pl.program_id(1)
    if tail_query:
        prev_from, cur_from, next_ok = BLOCK, BLOCK - N_META, True
        kv_prev = kvc_ref[0, :BLOCK]
        u_prev = jnp.zeros((POOL_HALO, POOL_DIM), F32)
        u_next = un_ref[0, :POOL_HALO]
        first_local = POOL_HALO + cur_from
        end_local = rows + 2 * POOL_HALO
    else:
        first = i == 0
        last = i == pl.num_programs(1) - 1
        prev_from = jnp.where(first, BLOCK - N_META, 0)
        cur_from, next_ok = 0, jnp.logical_not(last)
        kv_prev = jnp.where(first, kvt_ref[0], kvp_ref[0])
        u_prev = jnp.where(first, ut_ref[0, BLOCK - POOL_HALO:], up_ref[0, BLOCK - POOL_HALO:])
        u_next = jnp.where(last, 0.0, un_ref[0, :POOL_HALO])
        first_local = 0
        end_local = jnp.where(last, rows + POOL_HALO, rows + 2 * POOL_HALO)

    kv_all = jnp.concatenate([kv_prev, kvc_ref[0], kvn_ref[0]], axis=0)
    lane = lax.broadcasted_iota(jnp.int32, (1, BLOCK), 1)
    lo_half = lane < HEAD_DIM
    r_all = lax.broadcasted_iota(jnp.int32, (1, rows + 2 * BLOCK), 1)
    key_lo = jnp.where(prev_from < BLOCK, prev_from, BLOCK + cur_from)
    key_hi = jnp.where(next_ok, rows + 2 * BLOCK, rows + BLOCK)
    key_mask = jnp.where((r_all >= key_lo) & (r_all < key_hi), 0.0, NEG_INF).astype(F32)

    qi = lax.broadcasted_iota(jnp.int32, (BLOCK, 3 * BLOCK), 0)
    ki = lax.broadcasted_iota(jnp.int32, (BLOCK, 3 * BLOCK), 1)
    dist_i = jnp.abs(qi + BLOCK - ki)
    dist = dist_i.astype(F32)
    bias = [jnp.where(dist_i <= WINDOW, -(2.0 ** (-8.0 * (hd + 1) / N_HEADS)) * dist, NEG_INF)
            for hd in range(N_HEADS)]

    q_all = q_ref[0]
    zero = jnp.zeros((), BF16)
    attn_blocks = []
    for n in range(nblk):
        edge = tail_query or n == 0 or n == nblk - 1
        cm = key_mask[:, n * BLOCK:(n + 3) * BLOCK]
        pair_outs = [[] for _ in range(BLOCK // ATT_ROWS)]
        for kh in range(N_KV_HEADS):
            kd = kv_all[n * BLOCK:(n + 3) * BLOCK, kh * 2 * HEAD_DIM:(kh + 1) * 2 * HEAD_DIM]
            v_off = 2 * KV_DIM + kh * 2 * HEAD_DIM
            vd = kv_all[n * BLOCK:(n + 3) * BLOCK, v_off:v_off + 2 * HEAD_DIM]
            v_bd = jnp.concatenate([jnp.where(lo_half, vd, zero),
                                    jnp.where(lo_half, zero, vd)], axis=0)
            for pr in range(GQA_GROUP // 2):
                head0 = kh * GQA_GROUP + 2 * pr
                qp = q_all[n * BLOCK:(n + 1) * BLOCK, head0 * HEAD_DIM:(head0 + 2) * HEAD_DIM]
                qms = (jnp.where(lo_half, qp, zero), jnp.where(lo_half, zero, qp))
                for ri in range(BLOCK // ATT_ROWS):
                    qrows = slice(ri * ATT_ROWS, (ri + 1) * ATT_ROWS)
                    ps, inv = [], []
                    for side in range(2):
                        hd = head0 + side
                        s = lax.dot_general(qms[side][qrows], kd, (((1,), (1,)), ((), ())),
                                            preferred_element_type=F32)
                        s = s + bias[hd][qrows]
                        if edge:
                            s = s + cm
                        sk = sink_ref[hd]
                        m = jnp.maximum(jnp.max(s, axis=-1, keepdims=True), sk)
                        p = jnp.exp(s - m)
                        denom = jnp.sum(p, axis=-1, keepdims=True) + jnp.exp(sk - m)
                        ps.append(p.astype(BF16))
                        inv.append(1.0 / denom)
                    o = _dot(jnp.concatenate(ps, axis=1), v_bd)
                    pair_outs[ri].append(o * jnp.where(lo_half, inv[0], inv[1]))
        attn_blocks += [jnp.concatenate(po, axis=1) for po in pair_outs]
    attn = jnp.concatenate(attn_blocks, axis=0).astype(BF16)

    u_cur = uc_ref[0]
    u_ext = jnp.concatenate([u_prev, u_cur, u_next], axis=0)
    sums = _window_sums(u_ext, rows)
    j = lax.broadcasted_iota(jnp.int32, (rows, 1), 0) + POOL_HALO
    pooled = []
    for gi, w in enumerate(POOL_WINDOWS):
        cnt = jnp.minimum(j + (w - w // 2), end_local) - jnp.maximum(j - w // 2, first_local)
        cnt = jnp.maximum(cnt, 1).astype(F32)
        x = u_cur[:, gi * POOL_GROUP:(gi + 1) * POOL_GROUP]
        pg = (sums[gi] / cnt - x).astype(BF16)
        pooled.append(_dot(pg, pw_ref[gi]))
    y_pool = (jnp.concatenate(pooled, axis=1) * ps_ref[...]).astype(BF16)

    out = h_ref[0] + _dot(attn, wo_ref[:Q_DIM]) + _dot(y_pool, wo_ref[Q_DIM:])
    o_ref[0] = out


def _mix(h, q, kv, u, q_t, kv_t, u_t, sink, pool_w, pool_scale, w_out, *, tail_query):
    if tail_query:
        hq, qq, kvq, uq = h, q_t, kv_t, u_t
    else:
        hq, qq, kvq, uq = h, q, kv, u
    b, s, _ = hq.shape
    rows = min(MIX_ROWS, s)
    rb = rows // BLOCK
    nb_main = kv.shape[1] // BLOCK
    cur = lambda width: pl.BlockSpec((1, rows, width), lambda bi, i: (bi, i, 0))
    blk = lambda width, fn: pl.BlockSpec((1, BLOCK, width), fn)
    if tail_query:
        prev_map = lambda bi, i: (bi, 0, 0)
        next_map = lambda bi, i: (bi, 0, 0)
    else:
        prev_map = lambda bi, i: (bi, jnp.maximum(i * rb - 1, 0), 0)
        next_map = lambda bi, i: (bi, jnp.minimum((i + 1) * rb, nb_main - 1), 0)
    tail_map = lambda bi, i: (bi, 0, 0)
    kvw = 4 * KV_DIM
    in_specs = [
        pl.BlockSpec(memory_space=pltpu.SMEM),
        cur(D_MODEL), cur(Q_DIM),
        cur(kvw), blk(kvw, prev_map), blk(kvw, tail_map), blk(kvw, next_map),
        cur(POOL_DIM), blk(POOL_DIM, prev_map), blk(POOL_DIM, tail_map), blk(POOL_DIM, next_map),
        _resident((len(POOL_WINDOWS), POOL_GROUP, POOL_GROUP)),
        _resident((1, POOL_DIM)),
        _resident((Q_DIM + POOL_DIM, D_MODEL)),
    ]
    return pl.pallas_call(
        functools.partial(_mix_kernel, rows=rows, tail_query=tail_query),
        out_shape=jax.ShapeDtypeStruct((b, s, D_MODEL), F32),
        grid=(b, s // rows),
        in_specs=in_specs,
        out_specs=cur(D_MODEL),
        compiler_params=_params(2),
        name="mix_tail" if tail_query else "mix_main",
    )(sink, hq, qq, kvq, kv, kv_t, kv, uq, u, u_t, u, pool_w, pool_scale.reshape(1, POOL_DIM), w_out)


FOLD_SUB = FOLD_ROWS // BLOCK


def _fold_len(s):
    n_first = (s + N_META) // 2 + 1 + (BLOCK - N_META)
    return pl.cdiv(n_first, FOLD_ROWS) * FOLD_ROWS


def _fold_kernel(*refs, s):
    first = refs[:FOLD_SUB]
    tail_ref = refs[FOLD_SUB]
    win = refs[FOLD_SUB + 1:2 * FOLD_SUB + 2]
    cs_ref, o_ref = refs[2 * FOLD_SUB + 2:]
    j = pl.program_id(1)
    is_first = j == 0

    ri = lax.broadcasted_iota(jnp.int32, (BLOCK, 2 * BLOCK), 0)
    ki = lax.broadcasted_iota(jnp.int32, (BLOCK, 2 * BLOCK), 1)
    rev = jnp.where(ki == 2 * BLOCK - N_META - ri, 1.0, 0.0).astype(BF16)

    zero = jnp.zeros((), BF16)
    w = [r[0] for r in win]
    w[FOLD_SUB] = jnp.where(is_first, zero, w[FOLD_SUB])
    half = s // 2 + BLOCK - N_META // 2
    cs = cs_ref[...]
    for q in range(FOLD_SUB):
        x1 = first[q][0]
        if q == 0:
            x1 = jnp.where(is_first, tail_ref[0], x1)
        pair = jnp.concatenate([w[FOLD_SUB - 1 - q], w[FOLD_SUB - q]], axis=0)
        partner = _dot(rev, pair)
        c = (j * FOLD_SUB + q) * BLOCK + lax.broadcasted_iota(jnp.int32, (BLOCK, 1), 0)
        x1 = jnp.where(c <= half, x1.astype(F32), 0.0)
        partner = jnp.where(c < half, partner, 0.0)
        hs = (x1 + partner).astype(BF16)
        hd = (x1 - partner).astype(BF16)
        rows = slice(q * BLOCK, (q + 1) * BLOCK)
        for gi in range(N_FOURIER_GROUPS):
            lo = gi * FOURIER_GROUP
            e = _dot(hs[:, lo:lo + FOURIER_GROUP], cs[:, :FOURIER_GROUP])
            o = _dot(hd[:, lo:lo + FOURIER_GROUP], cs[:, FOURIER_GROUP:])
            o_ref[0, rows, lo:lo + FOURIER_GROUP] = e.astype(BF16)
            o_ref[0, rows, D_MODEL + lo:D_MODEL + lo + FOURIER_GROUP] = o.astype(BF16)


def _fold(xn, xn_tail, cs):
    b, s, _ = xn.shape
    nb = s // BLOCK
    lf = _fold_len(s)
    blk = lambda fn: pl.BlockSpec((1, BLOCK, D_MODEL), fn)
    first_specs = [blk(lambda bi, j, q=q: (bi, jnp.maximum(j * FOLD_SUB + q - 1, 0), 0))
                   for q in range(FOLD_SUB)]
    win_specs = [blk(lambda bi, j, m=m: (bi, jnp.minimum(nb - FOLD_SUB * (j + 1) + m, nb - 1), 0))
                 for m in range(FOLD_SUB + 1)]
    return pl.pallas_call(
        functools.partial(_fold_kernel, s=s),
        out_shape=jax.ShapeDtypeStruct((b, lf, 2 * D_MODEL), BF16),
        grid=(b, lf // FOLD_ROWS),
        in_specs=first_specs + [blk(lambda bi, j: (bi, 0, 0))] + win_specs
                 + [_resident((FOURIER_GROUP, 2 * FOURIER_GROUP))],
        out_specs=pl.BlockSpec((1, FOLD_ROWS, 2 * D_MODEL), lambda bi, j: (bi, j, 0)),
        compiler_params=_params(2),
        name="fold_dft",
    )(*([xn] * FOLD_SUB), xn_tail, *([xn] * (FOLD_SUB + 1)), cs)


def _twiddle_kernel(ca_ref, sa_ref, cb_ref, sb_ref, wc_ref, ws_ref):
    ca, sa = ca_ref[0], sa_ref[0]
    cb, sb = cb_ref[...], sb_ref[...]
    wc_ref[...] = (ca * cb - sa * sb).astype(BF16)
    ws_ref[...] = (-(sa * cb + ca * sb)).astype(BF16)


def _twiddles(s):
    l_seq = s + N_META
    lf = _fold_len(s)
    nblk = s // BLOCK
    pos = np.arange(lf, dtype=np.int64) - (BLOCK - N_META)
    col_ok = ((pos >= 0) & (pos <= l_seq // 2)).astype(np.float64)
    pos = np.where(col_ok > 0, pos, 0)
    row0 = N_META + BLOCK * np.arange(nblk, dtype=np.int64)
    theta = 2.0 * np.pi / l_seq
    alpha = theta * ((row0[:, None] * pos[None, :]) % l_seq)
    beta = theta * ((np.arange(BLOCK, dtype=np.int64)[:, None] * pos[None, :]) % l_seq)
    scale = col_ok / np.sqrt(float(l_seq) * FOURIER_GROUP)
    ca = jnp.asarray((np.cos(alpha) * scale).reshape(nblk, 1, lf), F32)
    sa = jnp.asarray((np.sin(alpha) * scale).reshape(nblk, 1, lf), F32)
    cb = jnp.asarray(np.cos(beta), F32)
    sb = jnp.asarray(np.sin(beta), F32)
    row_spec = pl.BlockSpec((1, 1, lf), lambda i: (i, 0, 0))
    out_spec = pl.BlockSpec((BLOCK, lf), lambda i: (i, 0))
    return pl.pallas_call(
        _twiddle_kernel,
        out_shape=(jax.ShapeDtypeStruct((s, lf), BF16),) * 2,
        grid=(nblk,),
        in_specs=[row_spec, row_spec, _resident((BLOCK, lf)), _resident((BLOCK, lf))],
        out_specs=(out_spec, out_spec),
        compiler_params=_params(1),
        name="twiddles",
    )(ca, sa, cb, sb)


def _seq_dft_kernel(wc_ref, ws_ref, eo_ref, h_ref, wo_ref, o_ref):
    y = _dot(wc_ref[...], eo_ref[0, :, :D_MODEL]) + _dot(ws_ref[...], eo_ref[0, :, D_MODEL:])
    o_ref[0] = h_ref[0] + _dot(y.astype(BF16), wo_ref[...])


def _seq_dft(wc, ws, eo, h, w_out):
    b, s, _ = h.shape
    lf = eo.shape[1]
    rows = DFT_ROWS
    w_spec = pl.BlockSpec((rows, lf), lambda bi, i: (i, 0))
    h_spec = pl.BlockSpec((1, rows, D_MODEL), lambda bi, i: (bi, i, 0))
    return pl.pallas_call(
        _seq_dft_kernel,
        out_shape=jax.ShapeDtypeStruct((b, s, D_MODEL), F32),
        grid=(b, s // rows),
        in_specs=[w_spec, w_spec,
                  pl.BlockSpec((1, lf, 2 * D_MODEL), lambda bi, i: (bi, 0, 0)),
                  h_spec, _resident((D_MODEL, D_MODEL))],
        out_specs=h_spec,
        compiler_params=_params(2),
        name="seq_dft",
    )(wc, ws, eo, h, w_out)


def _prep_weights(p):
    bf = lambda a: a.astype(BF16)
    w_in = p["w_in_ap"][0]
    wq = w_in[:, :Q_DIM] * (HEAD_DIM ** -0.5)
    wk = w_in[:, Q_DIM:Q_DIM + KV_DIM].reshape(D_MODEL, N_KV_HEADS, 1, HEAD_DIM)
    wv = w_in[:, Q_DIM + KV_DIM:Q_DIM + 2 * KV_DIM].reshape(D_MODEL, N_KV_HEADS, 1, HEAD_DIM)
    dup = lambda w: jnp.broadcast_to(w, (D_MODEL, N_KV_HEADS, 2, HEAD_DIM)).reshape(D_MODEL, 2 * KV_DIM)
    w_ext = jnp.concatenate([wq, dup(wk), dup(wv), w_in[:, Q_DIM + 2 * KV_DIM:]], axis=1)
    k = np.arange(FOURIER_GROUP, dtype=np.int64)
    ang = 2.0 * np.pi * ((k[:, None] * k[None, :]) % FOURIER_GROUP) / FOURIER_GROUP
    cs = jnp.asarray(np.concatenate([np.cos(ang), np.sin(ang)], axis=1), F32)
    return dict(
        ffn1=[(p["norm_ffn1"][l], bf(p["ffn1_w_gate"][l]), bf(p["ffn1_w_up"][l]), bf(p["ffn1_w_down"][l]))
              for l in range(2)],
        ffn2=[(p["norm_ffn2"][l], bf(p["ffn2_w_gate"][l]), bf(p["ffn2_w_up"][l]), bf(p["ffn2_w_down"][l]))
              for l in range(2)],
        norm_mix=p["norm_mix"],
        w_ext=bf(w_ext), sink=p["attn_sink"][0], pool_w=bf(p["pool_w"][0]),
        pool_scale=p["pool_scale"][0], w_out_ap=bf(p["w_out_ap"][0]),
        cs=bf(cs), w_out_f=bf(p["w_out_fourier"][0]), norm_final=p["norm_final"],
    )


def _trunk(x, meta, w):
    b, s, _ = x.shape
    flat = lambda a: a.reshape(-1, D_MODEL)
    hm = jnp.broadcast_to(meta[None], (b, N_META, D_MODEL))
    front_pad = lambda a: jnp.pad(a, ((0, 0), (BLOCK - N_META, 0), (0, 0)))

    h, q, kv, u = _ffn(flat(x), *w["ffn1"][0], g_post=w["norm_mix"][0], post="proj", w_proj=w["w_ext"])
    h, q, kv, u = (a.reshape(b, s, -1) for a in (h, q, kv, u))
    hm = _ffn(flat(hm), *w["ffn1"][0]).reshape(b, N_META, D_MODEL)
    hm_t = front_pad(hm)
    q_t, kv_t, u_t = _proj(hm_t, w["norm_mix"][0], w["w_ext"])
    mix_args = (w["sink"], w["pool_w"], w["pool_scale"], w["w_out_ap"])
    h_new = _mix(h, q, kv, u, q_t, kv_t, u_t, *mix_args, tail_query=False)
    hm = _mix(hm_t, q, kv, u, q_t, kv_t, u_t, *mix_args, tail_query=True)[:, BLOCK - N_META:]
    h = _ffn(flat(h_new), *w["ffn2"][0]).reshape(b, s, D_MODEL)
    hm = _ffn(flat(hm), *w["ffn2"][0]).reshape(b, N_META, D_MODEL)

    h, hn = _ffn(flat(h), *w["ffn1"][1], g_post=w["norm_mix"][1], post="normed")
    _, hmn = _ffn(flat(hm), *w["ffn1"][1], g_post=w["norm_mix"][1], post="normed")
    h = h.reshape(b, s, D_MODEL)
    hmn_t = front_pad(hmn.reshape(b, N_META, D_MODEL))
    eo = _fold(hn.reshape(b, s, D_MODEL), hmn_t, w["cs"])
    wc, ws = _twiddles(s)
    h = _seq_dft(wc, ws, eo, h, w["w_out_f"])
    return _ffn(flat(h), *w["ffn2"][1], g_post=w["norm_final"], post="final").reshape(b, s, D_MODEL)


def kernel(x_prompt, x_sample, meta_tokens, norm_ffn1, ffn1_w_gate, ffn1_w_up, ffn1_w_down, norm_mix, w_in_ap, attn_sink, pool_w, pool_scale, w_out_ap, w_out_fourier, norm_ffn2, ffn2_w_gate, ffn2_w_up, ffn2_w_down, norm_final):
    w = _prep_weights(dict(
        norm_ffn1=norm_ffn1, ffn1_w_gate=ffn1_w_gate, ffn1_w_up=ffn1_w_up, ffn1_w_down=ffn1_w_down,
        norm_mix=norm_mix, w_in_ap=w_in_ap, attn_sink=attn_sink, pool_w=pool_w, pool_scale=pool_scale,
        w_out_ap=w_out_ap, w_out_fourier=w_out_fourier, norm_ffn2=norm_ffn2, ffn2_w_gate=ffn2_w_gate,
        ffn2_w_up=ffn2_w_up, ffn2_w_down=ffn2_w_down, norm_final=norm_final))
    return (_trunk(x_prompt, meta_tokens, w), _trunk(x_sample, meta_tokens, w))
```
